```python
import jax, jax.numpy as jnp
from jax import lax
import numpy as np

D_MODEL = 2048
BATCH = 2
SEQ = 16384
DEPTH = 4

N_MIXERS = 2
HEAD_DIM = 128
ATTN_SLOTS = 16
ATTN_WIDTH = ATTN_SLOTS * HEAD_DIM
DILATION_GROUPS = ((128, 1), (512, 4), (2048, 16))
N_GROUPS = len(DILATION_GROUPS)
QKV_WIDTH = 3 * N_GROUPS * ATTN_WIDTH
ATTN_IN_WIDTH = QKV_WIDTH + ATTN_WIDTH
BLOCK_Q = 128
CONV_WIDTH = D_MODEL
CONV_K = 3
CONV_IN_WIDTH = 4 * CONV_WIDTH
NORM_EPS = 1e-6

kernel_name = "hybrid_dilated_swa_shortconv"


def alibi_slopes():
    n = N_GROUPS * ATTN_SLOTS
    s = 2.0 ** (-8.0 * np.arange(1, n + 1) / n)
    return s.reshape(N_GROUPS, ATTN_SLOTS).astype(np.float32)


def rmsnorm(x, g):
    xf = x.astype(jnp.float32)
    y = xf * lax.rsqrt(jnp.mean(xf * xf, axis=-1, keepdims=True) + NORM_EPS) * g.astype(jnp.float32)
    return y.astype(x.dtype)


def to_residues(t, d):
    b, s = t.shape[:2]
    rest = t.shape[2:]
    t = t.reshape((b, s // d, d) + rest)
    t = jnp.moveaxis(t, 2, 1)
    return t.reshape((b * d, s // d) + rest)


def from_residues(t, b, d):
    L = t.shape[1]
    rest = t.shape[2:]
    t = t.reshape((b, d, L) + rest)
    t = jnp.moveaxis(t, 1, 2)
    return t.reshape((b, L * d) + rest)


def band_attention(q, k, v, slopes, window, dilation):
    n, L, h, dh = q.shape
    nblk = -(-L // BLOCK_Q)
    lp = nblk * BLOCK_Q
    pad_end = lp - L
    qb = jnp.pad(q, ((0, 0), (0, pad_end), (0, 0), (0, 0))).reshape(n, nblk, BLOCK_Q, h, dh)

    def band(t):
        tp = jnp.pad(t, ((0, 0), (BLOCK_Q, pad_end), (0, 0), (0, 0))).reshape(n, nblk + 1, BLOCK_Q, h, dh)
        return jnp.concatenate([tp[:, :-1], tp[:, 1:]], axis=2)

    kb, vb = band(k), band(v)
    s = jnp.einsum('ncqhd,nckhd->nchqk', qb, kb, preferred_element_type=jnp.float32) * (dh ** -0.5)
    qi = jnp.arange(BLOCK_Q)[:, None]
    kj = jnp.arange(2 * BLOCK_Q)[None, :]
    dist = qi + BLOCK_Q - kj
    in_window = (dist >= 0) & (dist <= window)
    key_idx = jnp.arange(nblk)[:, None, None] * BLOCK_Q - BLOCK_Q + kj[None]
    valid = in_window[None] & (key_idx >= 0)
    bias = -(slopes[:, None, None] * (dilation * dist).astype(jnp.float32)[None])
    s = jnp.where(valid[None, :, None], s + bias[None, None], -jnp.inf)
    lse = jax.nn.logsumexp(s, axis=-1)
    p = jnp.exp(s - lse[..., None])
    o = jnp.einsum('nchqk,nckhd->ncqhd', p.astype(v.dtype), vb, preferred_element_type=jnp.float32)
    o = o.reshape(n, lp, h, dh)[:, :L]
    lse = jnp.swapaxes(lse, 2, 3).reshape(n, lp, h)[:, :L]
    return o, lse


def dilated_attention_mixer(x, w_in, w_out):
    b, s, _ = x.shape
    proj = x @ w_in
    qkv = proj[..., :QKV_WIDTH].reshape(b, s, 3, N_GROUPS, ATTN_SLOTS, HEAD_DIM)
    gate = proj[..., QKV_WIDTH:]
    slopes = jnp.asarray(alibi_slopes())
    outs, lses = [], []
    for g, (window, dil) in enumerate(DILATION_GROUPS):
        q = to_residues(qkv[:, :, 0, g], dil)
        k = to_residues(qkv[:, :, 1, g], dil)
        v = to_residues(qkv[:, :, 2, g], dil)
        o, lse = band_attention(q, k, v, slopes[g], window // dil, dil)
        outs.append(from_residues(o, b, dil))
        lses.append(from_residues(lse, b, dil))
    wts = jax.nn.softmax(jnp.stack(lses), axis=0)
    o = jnp.sum(wts[..., None] * jnp.stack(outs), axis=0)
    y = o.reshape(b, s, ATTN_WIDTH).astype(x.dtype) * jax.nn.silu(gate)
    return y @ w_out


def short_conv_mixer(x, w_in, conv_w, w_out):
    s = x.shape[1]
    proj = x @ w_in
    h, bg, cg, gate = jnp.split(proj, 4, axis=-1)
    u = cg * h
    up = jnp.pad(u, ((0, 0), (CONV_K - 1, 0), (0, 0)))
    conv = conv_w[0] * up[:, 0:s] + conv_w[1] * up[:, 1:s + 1] + conv_w[2] * up[:, 2:s + 2]
    y = bg * conv * jax.nn.silu(gate)
    return y @ w_out


def setup_inputs(seed: int = 0) -> dict:
    key = jax.random.key(seed)
    keys = jax.random.split(key, 1 + 8 * DEPTH)
    out = {"x": jax.random.normal(keys[0], (BATCH, SEQ, D_MODEL), jnp.float32)}
    ki = 1
    for i in range(DEPTH):
        kind = i % N_MIXERS
        out[f"l{i}_norm_pre"] = 1.0 + 0.1 * jax.random.normal(keys[ki], (D_MODEL,), jnp.float32); ki += 1
        if kind == 0:
            out[f"l{i}_w_in"] = jax.random.normal(keys[ki], (D_MODEL, ATTN_IN_WIDTH), jnp.float32) * D_MODEL ** -0.5; ki += 1
            out[f"l{i}_w_out"] = jax.random.normal(keys[ki], (ATTN_WIDTH, D_MODEL), jnp.float32) * ATTN_WIDTH ** -0.5; ki += 1
        else:
            out[f"l{i}_w_in"] = jax.random.normal(keys[ki], (D_MODEL, CONV_IN_WIDTH), jnp.float32) * D_MODEL ** -0.5; ki += 1
            out[f"l{i}_conv_w"] = jax.random.normal(keys[ki], (CONV_K, CONV_WIDTH), jnp.float32) * CONV_K ** -0.5; ki += 1
            out[f"l{i}_w_out"] = jax.random.normal(keys[ki], (CONV_WIDTH, D_MODEL), jnp.float32) * CONV_WIDTH ** -0.5; ki += 1
        out[f"l{i}_norm_post"] = 1.0 + 0.1 * jax.random.normal(keys[ki], (D_MODEL,), jnp.float32); ki += 1
    return out


def reference(x,
              l0_norm_pre, l0_w_in, l0_w_out, l0_norm_post,
              l1_norm_pre, l1_w_in, l1_conv_w, l1_w_out, l1_norm_post,
              l2_norm_pre, l2_w_in, l2_w_out, l2_norm_post,
              l3_norm_pre, l3_w_in, l3_conv_w, l3_w_out, l3_norm_post):
    layers = [
        (l0_norm_pre, (l0_w_in, l0_w_out), l0_norm_post),
        (l1_norm_pre, (l1_w_in, l1_conv_w, l1_w_out), l1_norm_post),
        (l2_norm_pre, (l2_w_in, l2_w_out), l2_norm_post),
        (l3_norm_pre, (l3_w_in, l3_conv_w, l3_w_out), l3_norm_post),
    ]
    for i in range(DEPTH):
        g_pre, params, g_post = layers[i]
        hN = rmsnorm(x, g_pre)
        if i % N_MIXERS == 0:
            y = dilated_attention_mixer(hN, *params)
        else:
            y = short_conv_mixer(hN, *params)
        x = x + rmsnorm(y, g_post)
    return x
```

```python
import functools
import math

import jax
import jax.numpy as jnp
import numpy as np
from jax import lax
from jax.experimental import pallas as pl
from jax.experimental.pallas import tpu as pltpu

D_MODEL = 2048
HEAD_DIM = 128
N_HEADS = 16
ATTN_WIDTH = N_HEADS * HEAD_DIM
DILATIONS = (1, 4, 16)
N_GROUPS = len(DILATIONS)
BLOCK = 128
CONV_K = 3
NORM_EPS = 1e-6
HALO_ROWS = 8
V7X_VMEM_LIMIT_BYTES = 56 * 1024 * 1024

BF16 = jnp.bfloat16
F32 = jnp.float32


def _alibi_slopes():
    n = N_GROUPS * N_HEADS
    s = 2.0 ** (-8.0 * np.arange(1, n + 1) / n)
    return s.reshape(N_GROUPS, N_HEADS).astype(np.float32)


def _params(*semantics):
    return pltpu.CompilerParams(dimension_semantics=semantics,
                                vmem_limit_bytes=V7X_VMEM_LIMIT_BYTES)


def _rms(v, g):
    return v * lax.rsqrt(jnp.mean(v * v, axis=-1, keepdims=True) + NORM_EPS) * g


def _silu(g):
    return g / (1.0 + jnp.exp(-g))


def _norm_kernel(x_ref, g_ref, o_ref):
    o_ref[...] = _rms(x_ref[...], g_ref[...]).astype(o_ref.dtype)


def _prenorm(x2, g, tm):
    t = x2.shape[0]
    return pl.pallas_call(
        _norm_kernel,
        grid=(t // tm,),
        in_specs=[pl.BlockSpec((tm, D_MODEL), lambda i: (i, 0)),
                  pl.BlockSpec((1, D_MODEL), lambda i: (0, 0))],
        out_specs=pl.BlockSpec((tm, D_MODEL), lambda i: (i, 0)),
        out_shape=jax.ShapeDtypeStruct((t, D_MODEL), BF16),
        compiler_params=_params("parallel"),
        name="prenorm",
    )(x2, g.reshape(1, D_MODEL))


def _mm_kernel(a_ref, w_ref, o_ref):
    o_ref[...] = jnp.dot(a_ref[...], w_ref[...],
                         preferred_element_type=F32).astype(o_ref.dtype)


def _proj_plain(hn, w, col0, ncols, tm, tn):
    t = hn.shape[0]
    cb0 = col0 // tn
    return pl.pallas_call(
        _mm_kernel,
        grid=(t // tm, ncols // tn),
        in_specs=[pl.BlockSpec((tm, D_MODEL), lambda i, j: (i, 0)),
                  pl.BlockSpec((D_MODEL, tn), lambda i, j: (0, cb0 + j))],
        out_specs=pl.BlockSpec((tm, tn), lambda i, j: (i, j)),
        out_shape=jax.ShapeDtypeStruct((t, ncols), BF16),
        compiler_params=_params("parallel", "arbitrary"),
        name="proj_plain",
    )(hn, w)


def _proj_group(hn3, w, g, dil, tm, tn):
    b, l, _ = hn3.shape
    per_seg = ATTN_WIDTH // tn

    def w_map(bi, r, u, j):
        return (0, ((j // per_seg) * N_GROUPS + g) * per_seg + j % per_seg)

    return pl.pallas_call(
        _mm_kernel,
        grid=(b, dil, l // tm, 3 * per_seg),
        in_specs=[pl.BlockSpec((None, tm, D_MODEL), lambda bi, r, u, j: (bi, u, r)),
                  pl.BlockSpec((D_MODEL, tn), w_map)],
        out_specs=pl.BlockSpec((None, None, tm, tn), lambda bi, r, u, j: (bi, r, u, j)),
        out_shape=jax.ShapeDtypeStruct((b, dil, l, 3 * ATTN_WIDTH), BF16),
        compiler_params=_params("parallel", "parallel", "parallel", "arbitrary"),
        name=f"proj_group{g}",
    )(hn3, w)


def _attn_kernel(q_ref, kp_ref, kc_ref, vp_ref, vc_ref, o_ref, lse_ref, *, dil, slopes):
    c = pl.program_id(2)
    qi = lax.broadcasted_iota(jnp.int32, (BLOCK, 2 * BLOCK), 0)
    kj = lax.broadcasted_iota(jnp.int32, (BLOCK, 2 * BLOCK), 1)
    dist = qi + BLOCK - kj
    valid = (dist >= 0) & (dist <= BLOCK) & ((kj >= BLOCK) | (c > 0))
    distf = (dil * dist).astype(F32)
    lane = lax.broadcasted_iota(jnp.int32, (BLOCK, HEAD_DIM), 1)
    lse_all = jnp.zeros((BLOCK, HEAD_DIM), F32)
    scale = HEAD_DIM ** -0.5
    for h in range(N_HEADS):
        hs = slice(h * HEAD_DIM, (h + 1) * HEAD_DIM)
        k = jnp.concatenate([kp_ref[:, hs], kc_ref[:, hs]], axis=0)
        v = jnp.concatenate([vp_ref[:, hs], vc_ref[:, hs]], axis=0)
        s = lax.dot_general(q_ref[:, hs], k, (((1,), (1,)), ((), ())),
                            preferred_element_type=F32) * scale
        s = jnp.where(valid, s - float(slopes[h]) * distf, -jnp.inf)
        m = jnp.max(s, axis=-1, keepdims=True)
        p = jnp.exp(s - m)
        l = jnp.sum(p, axis=-1, keepdims=True)
        o = jnp.dot(p.astype(BF16), v, preferred_element_type=F32) / l
        o_ref[:, hs] = o.astype(o_ref.dtype)
        lse_all = jnp.where(lane == h, m + jnp.log(l), lse_all)
    lse_ref[...] = lse_all


def _band_attention(qkv, g, dil):
    b, _, l, _ = qkv.shape
    blk = (None, None, BLOCK, ATTN_WIDTH)

    def cur(col):
        return pl.BlockSpec(blk, lambda bi, r, c: (bi, r, c, col))

    def prev(col):
        return pl.BlockSpec(blk, lambda bi, r, c: (bi, r, jnp.maximum(c - 1, 0), col))

    kern = functools.partial(_attn_kernel, dil=dil, slopes=_alibi_slopes()[g])
    return pl.pallas_call(
        kern,
        grid=(b, dil, l // BLOCK),
        in_specs=[cur(0), prev(1), cur(1), prev(2), cur(2)],
        out_specs=[pl.BlockSpec((None, BLOCK, ATTN_WIDTH), lambda bi, r, c: (bi, c, r)),
                   pl.BlockSpec((None, BLOCK, HEAD_DIM), lambda bi, r, c: (bi, c, r))],
        out_shape=[jax.ShapeDtypeStruct((b, l, dil * ATTN_WIDTH), BF16),
                   jax.ShapeDtypeStruct((b, l, dil * HEAD_DIM), F32)],
        compiler_params=_params("parallel", "parallel", "arbitrary"),
        name=f"band_attn{g}",
    )(qkv, qkv, qkv, qkv, qkv)


def _finish(y_ref, x_ref, wout_ref, gpost_ref, gnext_ref, xo_ref, hno_ref):
    out = jnp.dot(y_ref[...], wout_ref[...], preferred_element_type=F32)
    xn = x_ref[...] + _rms(out, gpost_ref[...])
    xo_ref[...] = xn
    hno_ref[...] = _rms(xn, gnext_ref[...]).astype(hno_ref.dtype)


def _attn_out_kernel(o0_ref, o1_ref, o2_ref, l0_ref, l1_ref, l2_ref, gate_ref, x_ref,
                     wout_ref, gpost_ref, gnext_ref, xo_ref, hno_ref, y_ref):
    lses = [l0_ref[...], l1_ref[...], l2_ref[...]]
    m = jnp.maximum(jnp.maximum(lses[0], lses[1]), lses[2])
    es = [jnp.exp(t - m) for t in lses]
    den = es[0] + es[1] + es[2]
    ws = [e / den for e in es]
    o_refs = (o0_ref, o1_ref, o2_ref)
    for h in range(N_HEADS):
        hs = slice(h * HEAD_DIM, (h + 1) * HEAD_DIM)
        acc = ws[0][:, h:h + 1] * o_refs[0][:, hs].astype(F32)
        for g in range(1, N_GROUPS):
            acc = acc + ws[g][:, h:h + 1] * o_refs[g][:, hs].astype(F32)
        y_ref[:, hs] = (acc * _silu(gate_ref[:, hs].astype(F32))).astype(y_ref.dtype)
    _finish(y_ref, x_ref, wout_ref, gpost_ref, gnext_ref, xo_ref, hno_ref)


def _row_spec(tm, width, col=0):
    return pl.BlockSpec((tm, width), lambda i: (i, col))


def _const_spec(shape):
    return pl.BlockSpec(shape, lambda i: (0,) * len(shape), pipeline_mode=pl.Buffered(1))


def _attn_epilogue(os_, lses, gate, x2, wout, gpost, gnext, tm):
    t = x2.shape[0]
    return pl.pallas_call(
        _attn_out_kernel,
        grid=(t // tm,),
        in_specs=[_row_spec(tm, ATTN_WIDTH)] * 3 + [_row_spec(tm, HEAD_DIM)] * 3
                 + [_row_spec(tm, ATTN_WIDTH), _row_spec(tm, D_MODEL),
                    _const_spec((ATTN_WIDTH, D_MODEL)), _const_spec((1, D_MODEL)),
                    _const_spec((1, D_MODEL))],
        out_specs=[_row_spec(tm, D_MODEL), _row_spec(tm, D_MODEL)],
        out_shape=[jax.ShapeDtypeStruct((t, D_MODEL), F32),
                   jax.ShapeDtypeStruct((t, D_MODEL), BF16)],
        scratch_shapes=[pltpu.VMEM((tm, ATTN_WIDTH), BF16)],
        compiler_params=_params("parallel"),
        name="attn_epilogue",
    )(*os_, *lses, gate, x2, wout, gpost.reshape(1, -1), gnext.reshape(1, -1))


def _conv_out_kernel(h_ref, b_ref, c_ref, gate_ref, hh_ref, ch_ref, x_ref, cw_ref,
                     wout_ref, gpost_ref, gnext_ref, xo_ref, hno_ref, y_ref, *, tiles_per_seq):
    tm = h_ref.shape[0]
    first = (pl.program_id(0) % tiles_per_seq) == 0
    u = c_ref[...].astype(F32) * h_ref[...].astype(F32)
    uh = ch_ref[...].astype(F32) * hh_ref[...].astype(F32)
    uh = jnp.where(first, 0.0, uh)
    ue = jnp.concatenate([uh, u], axis=0)
    um1 = ue[HALO_ROWS - 1:HALO_ROWS - 1 + tm]
    um2 = ue[HALO_ROWS - 2:HALO_ROWS - 2 + tm]
    cw = cw_ref[...]
    conv = cw[0:1] * um2 + cw[1:2] * um1 + cw[2:3] * u
    y = b_ref[...].astype(F32) * conv * _silu(gate_ref[...].astype(F32))
    y_ref[...] = y.astype(y_ref.dtype)
    _finish(y_ref, x_ref, wout_ref, gpost_ref, gnext_ref, xo_ref, hno_ref)


def _conv_epilogue(proj, x2, conv_w, wout, gpost, gnext, tm, seq):
    t = x2.shape[0]
    hb = tm // HALO_ROWS

    def halo(col):
        return pl.BlockSpec((HALO_ROWS, D_MODEL), lambda i: (jnp.maximum(i * hb - 1, 0), col))

    kern = functools.partial(_conv_out_kernel, tiles_per_seq=seq // tm)
    return pl.pallas_call(
        kern,
        grid=(t // tm,),
        in_specs=[_row_spec(tm, D_MODEL, 0), _row_spec(tm, D_MODEL, 1), _row_spec(tm, D_MODEL, 2),
                  _row_spec(tm, D_MODEL, 3), halo(0), halo(2), _row_spec(tm, D_MODEL),
                  _const_spec((CONV_K, D_MODEL)), _const_spec((D_MODEL, D_MODEL)),
                  _const_spec((1, D_MODEL)), _const_spec((1, D_MODEL))],
        out_specs=[_row_spec(tm, D_MODEL), _row_spec(tm, D_MODEL)],
        out_shape=[jax.ShapeDtypeStruct((t, D_MODEL), F32),
                   jax.ShapeDtypeStruct((t, D_MODEL), BF16)],
        scratch_shapes=[pltpu.VMEM((tm, D_MODEL), BF16)],
        compiler_params=_params("parallel"),
        name="conv_epilogue",
    )(proj, proj, proj, proj, proj, proj, x2, conv_w, wout,
      gpost.reshape(1, -1), gnext.reshape(1, -1))


def _attention_layer(x2, hn, w_in, w_out, gpost, gnext, batch, seq):
    t = x2.shape[0]
    w_in = w_in.astype(BF16)
    os_, lses = [], []
    for g, dil in enumerate(DILATIONS):
        l = seq // dil
        hn3 = hn.reshape(batch, l, dil * D_MODEL)
        qkv = _proj_group(hn3, w_in, g, dil, tm=min(1024, l), tn=1024)
        o, lse = _band_attention(qkv, g, dil)
        os_.append(o.reshape(t, ATTN_WIDTH))
        lses.append(lse.reshape(t, HEAD_DIM))
    gate = _proj_plain(hn, w_in, 3 * N_GROUPS * ATTN_WIDTH, ATTN_WIDTH, tm=min(1024, t), tn=1024)
    return _attn_epilogue(os_, lses, gate, x2, w_out.astype(BF16), gpost, gnext, tm=256)


def _conv_layer(x2, hn, w_in, conv_w, w_out, gpost, gnext, seq):
    t = x2.shape[0]
    proj = _proj_plain(hn, w_in.astype(BF16), 0, 4 * D_MODEL, tm=min(1024, t), tn=1024)
    return _conv_epilogue(proj, x2, conv_w, w_out.astype(BF16), gpost, gnext, tm=256, seq=seq)


def kernel(x, l0_norm_pre, l0_w_in, l0_w_out, l0_norm_post, l1_norm_pre, l1_w_in, l1_conv_w, l1_w_out, l1_norm_post, l2_norm_pre, l2_w_in, l2_w_out, l2_norm_post, l3_norm_pre, l3_w_in, l3_conv_w, l3_w_out, l3_norm_post):
    batch, seq, d = x.shape
    assert d == D_MODEL and seq % (DILATIONS[-1] * BLOCK) == 0
    t = batch * seq
    x2 = x.reshape(t, D_MODEL)
    hn = _prenorm(x2, l0_norm_pre, tm=min(1024, t))
    x2, hn = _attention_layer(x2, hn, l0_w_in, l0_w_out, l0_norm_post, l1_norm_pre, batch, seq)
    x2, hn = _conv_layer(x2, hn, l1_w_in, l1_conv_w, l1_w_out, l1_norm_post, l2_norm_pre, seq)
    x2, hn = _attention_layer(x2, hn, l2_w_in, l2_w_out, l2_norm_post, l3_norm_pre, batch, seq)
    x2, _ = _conv_layer(x2, hn, l3_w_in, l3_conv_w, l3_w_out, l3_norm_post, l3_norm_pre, seq)
    return x2.reshape(batch, seq, D_MODEL)
```

```python
import functools

import jax
import jax.numpy as jnp
import numpy as np
from jax import lax
from jax.experimental import pallas as pl
from jax.experimental.pallas import tpu as pltpu

D_MODEL = 2048
HEAD_DIM = 128
N_HEADS = 16
ATTN_WIDTH = N_HEADS * HEAD_DIM
DILATIONS = (1, 4, 16)
N_GROUPS = len(DILATIONS)
BLOCK = 128
CONV_K = 3
NORM_EPS = 1e-6
HALO_ROWS = 8
V7X_VMEM_LIMIT_BYTES = 56 * 1024 * 1024
EPILOGUE_TOKENS = 256
PROJ_ROWS = 1024
PROJ_COLS = 1024

BF16 = jnp.bfloat16
F32 = jnp.float32


def _alibi_slopes():
    n = N_GROUPS * N_HEADS
    s = 2.0 ** (-8.0 * np.arange(1, n + 1) / n)
    return s.reshape(N_GROUPS, N_HEADS).astype(np.float32)


def _to_token_order(tm, dil):
    p = np.zeros((tm, tm), np.float32)
    u, r = np.meshgrid(np.arange(tm // dil), np.arange(dil), indexing="ij")
    p[(u * dil + r).ravel(), (r * (tm // dil) + u).ravel()] = 1.0
    return p


def _params(*semantics):
    return pltpu.CompilerParams(dimension_semantics=semantics,
                                vmem_limit_bytes=V7X_VMEM_LIMIT_BYTES)


def _rms(v, g):
    return v * lax.rsqrt(jnp.mean(v * v, axis=-1, keepdims=True) + NORM_EPS) * g


def _silu(g):
    return g / (1.0 + jnp.exp(-g))


def _tok_spec(tm, width, tiles_per_seq, col=0):
    return pl.BlockSpec((tm, width), lambda b, i: (b * tiles_per_seq + i, col))


def _res_spec(tm, width, dil):
    return pl.BlockSpec((None, dil, tm // dil, width), lambda b, i: (b, 0, i, 0))


def _const_spec(shape):
    return pl.BlockSpec(shape, lambda b, i: (0,) * len(shape), pipeline_mode=pl.Buffered(1))


def _hn_out(batch, seq, tm, emit):
    t = batch * seq
    specs, shapes = [], []
    if emit >= 1:
        specs.append(_tok_spec(tm, D_MODEL, seq // tm))
        shapes.append(jax.ShapeDtypeStruct((t, D_MODEL), BF16))
    if emit >= 2:
        for dil in DILATIONS[1:]:
            specs.append(_res_spec(tm, D_MODEL, dil))
            shapes.append(jax.ShapeDtypeStruct((batch, dil, seq // dil, D_MODEL), BF16))
    return specs, shapes


def _perm_inputs(tm, emit_or_read):
    if not emit_or_read:
        return [], []
    mats = []
    for dil in DILATIONS[1:]:
        p = _to_token_order(tm, dil)
        mats += [p, p.T]
    return [jnp.asarray(m, BF16) for m in mats], [_const_spec((tm, tm))] * len(mats)


def _emit_hn(hn, perm_refs, out_refs):
    if not out_refs:
        return
    hb = hn.astype(BF16)
    out_refs[0][...] = hb
    for k, o_ref in enumerate(out_refs[1:]):
        to_res = perm_refs[2 * k + 1][...]
        res = jnp.dot(to_res, hb, preferred_element_type=F32).astype(BF16)
        dil, rows = o_ref.shape[0], o_ref.shape[1]
        for r in range(dil):
            o_ref[r] = res[r * rows:(r + 1) * rows]


def _norm_kernel(x_ref, g_ref, *refs):
    perm_refs, out_refs = refs[:4], refs[4:]
    _emit_hn(_rms(x_ref[...], g_ref[...]), perm_refs, out_refs)


def _prenorm(x2, g, batch, seq, tm):
    perms, perm_specs = _perm_inputs(tm, True)
    out_specs, out_shapes = _hn_out(batch, seq, tm, 2)
    return pl.pallas_call(
        _norm_kernel,
        grid=(batch, seq // tm),
        in_specs=[_tok_spec(tm, D_MODEL, seq // tm), _const_spec((1, D_MODEL))] + perm_specs,
        out_specs=out_specs,
        out_shape=out_shapes,
        compiler_params=_params("parallel", "parallel"),
        name="prenorm",
    )(x2, g.reshape(1, D_MODEL), *perms)


def _mm_kernel(a_ref, w_ref, o_ref):
    o_ref[...] = jnp.dot(a_ref[...], w_ref[...],
                         preferred_element_type=F32).astype(o_ref.dtype)


def _proj(hn, w, col_block, n_col_blocks, name):
    t = hn.shape[0]
    tm = min(PROJ_ROWS, t)
    return pl.pallas_call(
        _mm_kernel,
        grid=(t // tm, n_col_blocks),
        in_specs=[pl.BlockSpec((tm, D_MODEL), lambda i, j: (i, 0)),
                  pl.BlockSpec((D_MODEL, PROJ_COLS), lambda i, j: (0, col_block(j)))],
        out_specs=pl.BlockSpec((tm, PROJ_COLS), lambda i, j: (i, j)),
        out_shape=jax.ShapeDtypeStruct((t, n_col_blocks * PROJ_COLS), BF16),
        compiler_params=_params("parallel", "arbitrary"),
        name=name,
    )(hn, w)


def _attn_kernel(q_ref, kp_ref, kc_ref, vp_ref, vc_ref, o_ref, lse_ref, *, dil, slopes):
    c = pl.program_id(2)
    qi = lax.broadcasted_iota(jnp.int32, (BLOCK, 2 * BLOCK), 0)
    kj = lax.broadcasted_iota(jnp.int32, (BLOCK, 2 * BLOCK), 1)
    dist = qi + BLOCK - kj
    valid = (dist >= 0) & (dist <= BLOCK) & ((kj >= BLOCK) | (c > 0))
    distf = (dil * dist).astype(F32)
    lane = lax.broadcasted_iota(jnp.int32, (BLOCK, HEAD_DIM), 1)
    lse_all = jnp.zeros((BLOCK, HEAD_DIM), F32)
    scale = HEAD_DIM ** -0.5
    for h in range(N_HEADS):
        hs = slice(h * HEAD_DIM, (h + 1) * HEAD_DIM)
        k = jnp.concatenate([kp_ref[:, hs], kc_ref[:, hs]], axis=0)
        v = jnp.concatenate([vp_ref[:, hs], vc_ref[:, hs]], axis=0)
        s = lax.dot_general(q_ref[:, hs], k, (((1,), (1,)), ((), ())),
                            preferred_element_type=F32) * scale
        s = jnp.where(valid, s - float(slopes[h]) * distf, -jnp.inf)
        m = jnp.max(s, axis=-1, keepdims=True)
        p = jnp.exp(s - m)
        l = jnp.sum(p, axis=-1, keepdims=True)
        o = jnp.dot(p.astype(BF16), v, preferred_element_type=F32) / l
        o_ref[:, hs] = o.astype(o_ref.dtype)
        lse_all = jnp.where(lane == h, m + jnp.log(l), lse_all)
    lse_ref[...] = lse_all


def _band_attention(qkv, g, dil):
    b, _, l, _ = qkv.shape
    blk = (None, None, BLOCK, ATTN_WIDTH)

    def cur(col):
        return pl.BlockSpec(blk, lambda bi, r, c: (bi, r, c, col))

    def prev(col):
        return pl.BlockSpec(blk, lambda bi, r, c: (bi, r, jnp.maximum(c - 1, 0), col))

    kern = functools.partial(_attn_kernel, dil=dil, slopes=_alibi_slopes()[g])
    return pl.pallas_call(
        kern,
        grid=(b, dil, l // BLOCK),
        in_specs=[cur(0), prev(1), cur(1), prev(2), cur(2)],
        out_specs=[cur(0), pl.BlockSpec((None, None, BLOCK, HEAD_DIM),
                                        lambda bi, r, c: (bi, r, c, 0))],
        out_shape=[jax.ShapeDtypeStruct((b, dil, l, ATTN_WIDTH), BF16),
                   jax.ShapeDtypeStruct((b, dil, l, HEAD_DIM), F32)],
        compiler_params=_params("parallel", "parallel", "arbitrary"),
        name=f"band_attn{g}",
    )(qkv, qkv, qkv, qkv, qkv)


def _finish(y_ref, x_ref, wout_ref, gpost_ref, gnext_ref, perm_refs, xo_ref, hn_refs):
    out = jnp.dot(y_ref[...], wout_ref[...], preferred_element_type=F32)
    xn = x_ref[...] + _rms(out, gpost_ref[...])
    xo_ref[...] = xn
    if hn_refs:
        _emit_hn(_rms(xn, gnext_ref[...]), perm_refs, hn_refs)


def _attn_out_kernel(o0_ref, o1_ref, o2_ref, l0_ref, l1_ref, l2_ref, gate_ref, x_ref,
                     wout_ref, gpost_ref, gnext_ref, *refs):
    perm_refs, xo_ref, hn_refs, y_ref = refs[:4], refs[4], refs[5:-1], refs[-1]
    tm = x_ref.shape[0]
    lses = [l0_ref[...], l1_ref[...], l2_ref[...]]
    m = jnp.maximum(jnp.maximum(lses[0], lses[1]), lses[2])
    es = [jnp.exp(t - m) for t in lses]
    den = es[0] + es[1] + es[2]
    ws = [e / den for e in es]
    os_ = [o0_ref[...].astype(F32)]
    for k, o_ref in enumerate((o1_ref, o2_ref)):
        res = o_ref[...].reshape(tm, ATTN_WIDTH)
        os_.append(jnp.dot(perm_refs[2 * k][...], res, preferred_element_type=F32))
    for h in range(N_HEADS):
        hs = slice(h * HEAD_DIM, (h + 1) * HEAD_DIM)
        acc = ws[0][:, h:h + 1] * os_[0][:, hs]
        for g in range(1, N_GROUPS):
            acc = acc + ws[g][:, h:h + 1] * os_[g][:, hs]
        y_ref[:, hs] = (acc * _silu(gate_ref[:, hs].astype(F32))).astype(y_ref.dtype)
    _finish(y_ref, x_ref, wout_ref, gpost_ref, gnext_ref, perm_refs, xo_ref, hn_refs)


def _attn_epilogue(os_, lses, gate, x2, wout, gpost, gnext, batch, seq, emit):
    tm = EPILOGUE_TOKENS
    tps = seq // tm
    t = batch * seq
    perms, perm_specs = _perm_inputs(tm, True)
    hn_specs, hn_shapes = _hn_out(batch, seq, tm, emit)
    return pl.pallas_call(
        _attn_out_kernel,
        grid=(batch, tps),
        in_specs=[_tok_spec(tm, ATTN_WIDTH, tps), _res_spec(tm, ATTN_WIDTH, DILATIONS[1]),
                  _res_spec(tm, ATTN_WIDTH, DILATIONS[2])]
                 + [_tok_spec(tm, HEAD_DIM, tps)] * 3
                 + [_tok_spec(tm, ATTN_WIDTH, tps), _tok_spec(tm, D_MODEL, tps),
                    _const_spec((ATTN_WIDTH, D_MODEL)), _const_spec((1, D_MODEL)),
                    _const_spec((1, D_MODEL))] + perm_specs,
        out_specs=[_tok_spec(tm, D_MODEL, tps)] + hn_specs,
        out_shape=[jax.ShapeDtypeStruct((t, D_MODEL), F32)] + hn_shapes,
        scratch_shapes=[pltpu.VMEM((tm, ATTN_WIDTH), BF16)],
        compiler_params=_params("parallel", "parallel"),
        name="attn_epilogue",
    )(*os_, *lses, gate, x2, wout, gpost.reshape(1, -1), gnext.reshape(1, -1), *perms)


def _conv_out_kernel(h_ref, b_ref, c_ref, gate_ref, hh_ref, ch_ref, x_ref, cw_ref,
                     wout_ref, gpost_ref, gnext_ref, *refs, n_perm):
    perm_refs, xo_ref = refs[:n_perm], refs[n_perm]
    hn_refs, y_ref = refs[n_perm + 1:-1], refs[-1]
    tm = h_ref.shape[0]
    first = pl.program_id(1) == 0
    u = c_ref[...].astype(F32) * h_ref[...].astype(F32)
    uh = ch_ref[...].astype(F32) * hh_ref[...].astype(F32)
    uh = jnp.where(first, 0.0, uh)
    ue = jnp.concatenate([uh, u], axis=0)
    um1 = ue[HALO_ROWS - 1:HALO_ROWS - 1 + tm]
    um2 = ue[HALO_ROWS - 2:HALO_ROWS - 2 + tm]
    cw = cw_ref[...]
    conv = cw[0:1] * um2 + cw[1:2] * um1 + cw[2:3] * u
    y = b_ref[...].astype(F32) * conv * _silu(gate_ref[...].astype(F32))
    y_ref[...] = y.astype(y_ref.dtype)
    _finish(y_ref, x_ref, wout_ref, gpost_ref, gnext_ref, perm_refs, xo_ref, hn_refs)


def _conv_epilogue(proj, x2, conv_w, wout, gpost, gnext, batch, seq, emit):
    tm = EPILOGUE_TOKENS
    tps = seq // tm
    t = batch * seq
    hb = tm // HALO_ROWS

    def halo(col):
        return pl.BlockSpec((HALO_ROWS, D_MODEL),
                            lambda b, i: (jnp.maximum((b * tps + i) * hb - 1, 0), col))

    perms, perm_specs = _perm_inputs(tm, emit >= 2)
    hn_specs, hn_shapes = _hn_out(batch, seq, tm, emit)
    kern = functools.partial(_conv_out_kernel, n_perm=len(perms))
    return pl.pallas_call(
        kern,
        grid=(batch, tps),
        in_specs=[_tok_spec(tm, D_MODEL, tps, c) for c in range(4)]
                 + [halo(0), halo(2), _tok_spec(tm, D_MODEL, tps),
                    _const_spec((CONV_K, D_MODEL)), _const_spec((D_MODEL, D_MODEL)),
                    _const_spec((1, D_MODEL)), _const_spec((1, D_MODEL))] + perm_specs,
        out_specs=[_tok_spec(tm, D_MODEL, tps)] + hn_specs,
        out_shape=[jax.ShapeDtypeStruct((t, D_MODEL), F32)] + hn_shapes,
        scratch_shapes=[pltpu.VMEM((tm, D_MODEL), BF16)],
        compiler_params=_params("parallel", "arbitrary"),
        name="conv_epilogue",
    )(proj, proj, proj, proj, proj, proj, x2, conv_w, wout,
      gpost.reshape(1, -1), gnext.reshape(1, -1), *perms)


def _attention_layer(x2, hns, w_in, w_out, gpost, gnext, batch, seq, emit):
    t = x2.shape[0]
    w_in = w_in.astype(BF16)
    per_seg = ATTN_WIDTH // PROJ_COLS
    os_, lses = [], []
    for g, dil in enumerate(DILATIONS):
        def col_block(j, g=g):
            return ((j // per_seg) * N_GROUPS + g) * per_seg + j % per_seg
        qkv = _proj(hns[g].reshape(t, D_MODEL), w_in, col_block, 3 * per_seg, f"proj_group{g}")
        o, lse = _band_attention(qkv.reshape(batch, dil, seq // dil, 3 * ATTN_WIDTH), g, dil)
        os_.append(o.reshape(t, ATTN_WIDTH) if dil == 1 else o)
        lses.append(jnp.swapaxes(lse, 1, 2).reshape(t, HEAD_DIM))
    gate0 = 3 * N_GROUPS * per_seg
    gate = _proj(hns[0], w_in, lambda j: gate0 + j, per_seg, "proj_gate")
    return _attn_epilogue(os_, lses, gate, x2, w_out.astype(BF16), gpost, gnext, batch, seq, emit)


def _conv_layer(x2, hn, w_in, conv_w, w_out, gpost, gnext, batch, seq, emit):
    proj = _proj(hn, w_in.astype(BF16), lambda j: j, 4 * D_MODEL // PROJ_COLS, "proj_conv")
    return _conv_epilogue(proj, x2, conv_w, w_out.astype(BF16), gpost, gnext, batch, seq, emit)


def kernel(x, l0_norm_pre, l0_w_in, l0_w_out, l0_norm_post, l1_norm_pre, l1_w_in, l1_conv_w, l1_w_out, l1_norm_post, l2_norm_pre, l2_w_in, l2_w_out, l2_norm_post, l3_norm_pre, l3_w_in, l3_conv_w, l3_w_out, l3_norm_post):
    batch, seq, d = x.shape
    assert d == D_MODEL and seq % (DILATIONS[-1] * BLOCK) == 0
    t = batch * seq
    x2 = x.reshape(t, D_MODEL)
    hns = _prenorm(x2, l0_norm_pre, batch, seq, EPILOGUE_TOKENS)
    x2, hn = _attention_layer(x2, hns, l0_w_in, l0_w_out, l0_norm_post, l1_norm_pre, batch, seq, 1)
    x2, *hns = _conv_layer(x2, hn, l1_w_in, l1_conv_w, l1_w_out, l1_norm_post, l2_norm_pre,
                           batch, seq, 2)
    x2, hn = _attention_layer(x2, hns, l2_w_in, l2_w_out, l2_norm_post, l3_norm_pre, batch, seq, 1)
    (x2,) = _conv_layer(x2, hn, l3_w_in, l3_conv_w, l3_w_out, l3_norm_post, l3_norm_pre,
                        batch, seq, 0)
    return x2.reshape(batch, seq, D_MODEL)
```

```python
import functools

import jax
import jax.numpy as jnp
import numpy as np
from jax import lax
from jax.experimental import pallas as pl
from jax.experimental.pallas import tpu as pltpu

D_MODEL = 2048
HEAD_DIM = 128
N_HEADS = 16
ATTN_WIDTH = N_HEADS * HEAD_DIM
DILATIONS = (1, 4, 16)
N_GROUPS = len(DILATIONS)
BLOCK = 128
CONV_K = 3
NORM_EPS = 1e-6
HALO_ROWS = 8
V7X_VMEM_LIMIT_BYTES = 56 * 1024 * 1024
EPILOGUE_TOKENS = 256
PROJ_ROWS = 1024
PROJ_COLS = 1024
ATTN_ROWS = 1024
HEADS_PER_STEP = 2

LOG2E = 1.4426950408889634
Q_SCALE = HEAD_DIM ** -0.5 * LOG2E

BF16 = jnp.bfloat16
F32 = jnp.float32


def _alibi_slopes():
    n = N_GROUPS * N_HEADS
    s = 2.0 ** (-8.0 * np.arange(1, n + 1) / n)
    return s.reshape(N_GROUPS, N_HEADS).astype(np.float32)


def _to_token_order(tm, dil):
    p = np.zeros((tm, tm), np.float32)
    u, r = np.meshgrid(np.arange(tm // dil), np.arange(dil), indexing="ij")
    p[(u * dil + r).ravel(), (r * (tm // dil) + u).ravel()] = 1.0
    return p


def _params(*semantics):
    return pltpu.CompilerParams(dimension_semantics=semantics,
                                vmem_limit_bytes=V7X_VMEM_LIMIT_BYTES)


def _rms(v, g):
    return v * lax.rsqrt(jnp.mean(v * v, axis=-1, keepdims=True) + NORM_EPS) * g


def _silu(g):
    return g / (1.0 + jnp.exp(-g))


def _tok_spec(tm, width, tiles_per_seq, col=0):
    return pl.BlockSpec((tm, width), lambda b, i: (b * tiles_per_seq + i, col))


def _res_spec(tm, width, dil):
    return pl.BlockSpec((None, dil, tm // dil, width), lambda b, i: (b, 0, i, 0))


def _const_spec(shape):
    return pl.BlockSpec(shape, lambda b, i: (0,) * len(shape), pipeline_mode=pl.Buffered(1))


def _hn_out(batch, seq, tm, emit):
    t = batch * seq
    specs, shapes = [], []
    if emit >= 1:
        specs.append(_tok_spec(tm, D_MODEL, seq // tm))
        shapes.append(jax.ShapeDtypeStruct((t, D_MODEL), BF16))
    if emit >= 2:
        for dil in DILATIONS[1:]:
            specs.append(_res_spec(tm, D_MODEL, dil))
            shapes.append(jax.ShapeDtypeStruct((batch, dil, seq // dil, D_MODEL), BF16))
    return specs, shapes


def _perm_inputs(tm, emit_or_read):
    if not emit_or_read:
        return [], []
    mats = []
    for dil in DILATIONS[1:]:
        p = _to_token_order(tm, dil)
        mats += [p, p.T]
    return [jnp.asarray(m, BF16) for m in mats], [_const_spec((tm, tm))] * len(mats)


def _emit_hn(hn, perm_refs, out_refs):
    if not out_refs:
        return
    hb = hn.astype(BF16)
    out_refs[0][...] = hb
    for k, o_ref in enumerate(out_refs[1:]):
        to_res = perm_refs[2 * k + 1][...]
        res = jnp.dot(to_res, hb, preferred_element_type=F32).astype(BF16)
        dil, rows = o_ref.shape[0], o_ref.shape[1]
        for r in range(dil):
            o_ref[r] = res[r * rows:(r + 1) * rows]


def _norm_kernel(x_ref, g_ref, *refs):
    perm_refs, out_refs = refs[:4], refs[4:]
    _emit_hn(_rms(x_ref[...], g_ref[...]), perm_refs, out_refs)


def _prenorm(x2, g, batch, seq, tm):
    perms, perm_specs = _perm_inputs(tm, True)
    out_specs, out_shapes = _hn_out(batch, seq, tm, 2)
    return pl.pallas_call(
        _norm_kernel,
        grid=(batch, seq // tm),
        in_specs=[_tok_spec(tm, D_MODEL, seq // tm), _const_spec((1, D_MODEL))] + perm_specs,
        out_specs=out_specs,
        out_shape=out_shapes,
        compiler_params=_params("parallel", "parallel"),
        name="prenorm",
    )(x2, g.reshape(1, D_MODEL), *perms)


def _mm_kernel(a_ref, w_ref, o_ref):
    o_ref[...] = jnp.dot(a_ref[...], w_ref[...],
                         preferred_element_type=F32).astype(o_ref.dtype)


def _proj(hn, w, col_block, n_col_blocks, name):
    t = hn.shape[0]
    tm = min(PROJ_ROWS, t)
    return pl.pallas_call(
        _mm_kernel,
        grid=(t // tm, n_col_blocks),
        in_specs=[pl.BlockSpec((tm, D_MODEL), lambda i, j: (i, 0)),
                  pl.BlockSpec((D_MODEL, PROJ_COLS), lambda i, j: (0, col_block(j)))],
        out_specs=pl.BlockSpec((tm, PROJ_COLS), lambda i, j: (i, j)),
        out_shape=jax.ShapeDtypeStruct((t, n_col_blocks * PROJ_COLS), BF16),
        compiler_params=_params("parallel", "arbitrary"),
        name=name,
    )(hn, w)


def _attn_bias_table(g, dil):
    qi = np.arange(BLOCK)[:, None]
    kj = np.arange(2 * BLOCK)[None, :]
    dist = qi + BLOCK - kj
    in_window = (dist >= 0) & (dist <= BLOCK)
    slopes = _alibi_slopes()[g].astype(np.float64) * LOG2E
    bias = -(slopes[:, None, None] * (dil * dist)[None].astype(np.float64))
    general = np.where(in_window[None], bias, -np.inf)
    first = np.where((kj >= BLOCK)[None], general, -np.inf)
    return np.stack([first, general]).astype(np.float32)


def _attend(q, kp, kc, vp, vc, bias):
    k = jnp.concatenate([kp, kc], axis=0)
    v = jnp.concatenate([vp, vc], axis=0)
    s = lax.dot_general(q, k, (((1,), (1,)), ((), ())), preferred_element_type=F32) + bias
    m = jnp.max(s, axis=-1, keepdims=True)
    p = jnp.exp2(s - m)
    l = jnp.sum(p, axis=-1, keepdims=True)
    return jnp.dot(p.astype(BF16), v, preferred_element_type=F32), m, l


def _proj_attn_kernel(hn_ref, w_ref, bias_ref, o_ref, m_ref, l_ref, slab_ref, halo_ref, *,
                      tiles_per_seq):
    i, hp = pl.program_id(0), pl.program_id(1)
    tr = hn_ref.shape[0]
    pair = HEADS_PER_STEP * HEAD_DIM

    @pl.when((i == 0) & (hp == 0))
    def _():
        slab_ref[...] = jnp.zeros_like(slab_ref)
        halo_ref[...] = jnp.zeros_like(halo_ref)

    @pl.when(hp == 0)
    def _():
        m_ref[...] = jnp.zeros_like(m_ref)
        l_ref[...] = jnp.ones_like(l_ref)

    first = ((i - 1) % tiles_per_seq) == 0
    lane = lax.broadcasted_iota(jnp.int32, (BLOCK, HEAD_DIM), 1)

    def cols(part, hh):
        return slice(part * pair + hh * HEAD_DIM, part * pair + (hh + 1) * HEAD_DIM)

    for qb in range(tr // BLOCK):
        rows = slice(qb * BLOCK, (qb + 1) * BLOCK)
        prows = slice((qb - 1) * BLOCK, qb * BLOCK)
        m_t, l_t = m_ref[rows, :], l_ref[rows, :]
        for hh in range(HEADS_PER_STEP):
            if qb == 0:
                kp, vp = halo_ref[hp, :, cols(0, hh)], halo_ref[hp, :, cols(1, hh)]
                bias = jnp.where(first, bias_ref[0, hh], bias_ref[1, hh])
            else:
                kp, vp = slab_ref[hp, prows, cols(1, hh)], slab_ref[hp, prows, cols(2, hh)]
                bias = bias_ref[1, hh]
            o, m, l = _attend(slab_ref[hp, rows, cols(0, hh)], kp, slab_ref[hp, rows, cols(1, hh)],
                              vp, slab_ref[hp, rows, cols(2, hh)], bias)
            o_ref[rows, hh * HEAD_DIM:(hh + 1) * HEAD_DIM] = o.astype(o_ref.dtype)
            head = hp * HEADS_PER_STEP + hh
            m_t = jnp.where(lane == head, m, m_t)
            l_t = jnp.where(lane == head, l, l_t)
        m_ref[rows, :] = m_t
        l_ref[rows, :] = l_t
    halo_ref[hp] = slab_ref[hp, tr - BLOCK:tr, pair:3 * pair]
    slab_ref[hp] = jnp.dot(hn_ref[...], w_ref[...],
                           preferred_element_type=F32).astype(slab_ref.dtype)


def _proj_attention(hn, w_groups, g, dil, seq):
    t = hn.shape[0]
    l_sub = seq // dil
    tr = min(ATTN_ROWS, l_sub)
    n_tiles = t // tr
    pair = HEADS_PER_STEP * HEAD_DIM
    n_pairs = N_HEADS // HEADS_PER_STEP
    bias = jnp.asarray(_attn_bias_table(g, dil))

    def tile_done(i, hp):
        return (jnp.maximum(i - 1, 0), 0)

    kern = functools.partial(_proj_attn_kernel, tiles_per_seq=l_sub // tr)
    return pl.pallas_call(
        kern,
        grid=(n_tiles + 1, n_pairs),
        in_specs=[pl.BlockSpec((tr, D_MODEL), lambda i, hp: (jnp.minimum(i, n_tiles - 1), 0)),
                  pl.BlockSpec((None, D_MODEL, 3 * pair), lambda i, hp: (g, 0, hp)),
                  pl.BlockSpec((2, HEADS_PER_STEP, BLOCK, 2 * BLOCK),
                               lambda i, hp: (0, hp, 0, 0))],
        out_specs=[pl.BlockSpec((tr, pair),
                                lambda i, hp: (jnp.maximum(i - 1, 0), jnp.where(i == 0, 0, hp))),
                   pl.BlockSpec((tr, HEAD_DIM), tile_done),
                   pl.BlockSpec((tr, HEAD_DIM), tile_done)],
        out_shape=[jax.ShapeDtypeStruct((t, ATTN_WIDTH), BF16),
                   jax.ShapeDtypeStruct((t, HEAD_DIM), F32),
                   jax.ShapeDtypeStruct((t, HEAD_DIM), F32)],
        scratch_shapes=[pltpu.VMEM((n_pairs, tr, 3 * pair), BF16),
                        pltpu.VMEM((n_pairs, BLOCK, 2 * pair), BF16)],
        compiler_params=_params("arbitrary", "arbitrary"),
        name=f"proj_attn{g}",
    )(hn, w_groups, bias)


def _finish(y_ref, x_ref, wout_ref, gpost_ref, gnext_ref, perm_refs, xo_ref, hn_refs):
    out = jnp.dot(y_ref[...], wout_ref[...], preferred_element_type=F32)
    xn = x_ref[...] + _rms(out, gpost_ref[...])
    xo_ref[...] = xn
    if hn_refs:
        _emit_hn(_rms(xn, gnext_ref[...]), perm_refs, hn_refs)


def _attn_out_kernel(o0_ref, o1_ref, o2_ref, m0_ref, m1_ref, m2_ref, l0_ref, l1_ref, l2_ref,
                     gate_ref, x_ref, wout_ref, gpost_ref, gnext_ref, *refs):
    perm_refs, xo_ref, hn_refs, y_ref = refs[:4], refs[4], refs[5:-1], refs[-1]
    tm = x_ref.shape[0]
    ms = [m0_ref[...], m1_ref[...], m2_ref[...]]
    m = jnp.maximum(jnp.maximum(ms[0], ms[1]), ms[2])
    es = [jnp.exp2(t - m) for t in ms]
    den = es[0] * l0_ref[...] + es[1] * l1_ref[...] + es[2] * l2_ref[...]
    ws = [e / den for e in es]
    os_ = [o0_ref[...].astype(F32)]
    for k, o_ref in enumerate((o1_ref, o2_ref)):
        res = o_ref[...].reshape(tm, ATTN_WIDTH)
        os_.append(jnp.dot(perm_refs[2 * k][...], res, preferred_element_type=F32))
    for h in range(N_HEADS):
        hs = slice(h * HEAD_DIM, (h + 1) * HEAD_DIM)
        acc = ws[0][:, h:h + 1] * os_[0][:, hs]
        for g in range(1, N_GROUPS):
            acc = acc + ws[g][:, h:h + 1] * os_[g][:, hs]
        y_ref[:, hs] = (acc * _silu(gate_ref[:, hs].astype(F32))).astype(y_ref.dtype)
    _finish(y_ref, x_ref, wout_ref, gpost_ref, gnext_ref, perm_refs, xo_ref, hn_refs)


def _attn_epilogue(os_, stats, gate, x2, wout, gpost, gnext, batch, seq, emit):
    tm = EPILOGUE_TOKENS
    tps = seq // tm
    t = batch * seq
    perms, perm_specs = _perm_inputs(tm, True)
    hn_specs, hn_shapes = _hn_out(batch, seq, tm, emit)
    return pl.pallas_call(
        _attn_out_kernel,
        grid=(batch, tps),
        in_specs=[_tok_spec(tm, ATTN_WIDTH, tps), _res_spec(tm, ATTN_WIDTH, DILATIONS[1]),
                  _res_spec(tm, ATTN_WIDTH, DILATIONS[2])]
                 + [_tok_spec(tm, HEAD_DIM, tps)] * (2 * N_GROUPS)
                 + [_tok_spec(tm, ATTN_WIDTH, tps), _tok_spec(tm, D_MODEL, tps),
                    _const_spec((ATTN_WIDTH, D_MODEL)), _const_spec((1, D_MODEL)),
                    _const_spec((1, D_MODEL))] + perm_specs,
        out_specs=[_tok_spec(tm, D_MODEL, tps)] + hn_specs,
        out_shape=[jax.ShapeDtypeStruct((t, D_MODEL), F32)] + hn_shapes,
        scratch_shapes=[pltpu.VMEM((tm, ATTN_WIDTH), BF16)],
        compiler_params=_params("parallel", "parallel"),
        name="attn_epilogue",
    )(*os_, *stats, gate, x2, wout, gpost.reshape(1, -1), gnext.reshape(1, -1), *perms)


def _conv_out_kernel(h_ref, b_ref, c_ref, gate_ref, hh_ref, ch_ref, x_ref, cw_ref,
                     wout_ref, gpost_ref, gnext_ref, *refs, n_perm):
    perm_refs, xo_ref = refs[:n_perm], refs[n_perm]
    hn_refs, y_ref = refs[n_perm + 1:-1], refs[-1]
    tm = h_ref.shape[0]
    first = pl.program_id(1) == 0
    u = c_ref[...].astype(F32) * h_ref[...].astype(F32)
    uh = ch_ref[...].astype(F32) * hh_ref[...].astype(F32)
    uh = jnp.where(first, 0.0, uh)
    ue = jnp.concatenate([uh, u], axis=0)
    um1 = ue[HALO_ROWS - 1:HALO_ROWS - 1 + tm]
    um2 = ue[HALO_ROWS - 2:HALO_ROWS - 2 + tm]
    cw = cw_ref[...]
    conv = cw[0:1] * um2 + cw[1:2] * um1 + cw[2:3] * u
    y = b_ref[...].astype(F32) * conv * _silu(gate_ref[...].astype(F32))
    y_ref[...] = y.astype(y_ref.dtype)
    _finish(y_ref, x_ref, wout_ref, gpost_ref, gnext_ref, perm_refs, xo_ref, hn_refs)


def _conv_epilogue(proj, x2, conv_w, wout, gpost, gnext, batch, seq, emit):
    tm = EPILOGUE_TOKENS
    tps = seq // tm
    t = batch * seq
    hb = tm // HALO_ROWS

    def halo(col):
        return pl.BlockSpec((HALO_ROWS, D_MODEL),
                            lambda b, i: (jnp.maximum((b * tps + i) * hb - 1, 0), col))

    perms, perm_specs = _perm_inputs(tm, emit >= 2)
    hn_specs, hn_shapes = _hn_out(batch, seq, tm, emit)
    kern = functools.partial(_conv_out_kernel, n_perm=len(perms))
    return pl.pallas_call(
        kern,
        grid=(batch, tps),
        in_specs=[_tok_spec(tm, D_MODEL, tps, c) for c in range(4)]
                 + [halo(0), halo(2), _tok_spec(tm, D_MODEL, tps),
                    _const_spec((CONV_K, D_MODEL)), _const_spec((D_MODEL, D_MODEL)),
                    _const_spec((1, D_MODEL)), _const_spec((1, D_MODEL))] + perm_specs,
        out_specs=[_tok_spec(tm, D_MODEL, tps)] + hn_specs,
        out_shape=[jax.ShapeDtypeStruct((t, D_MODEL), F32)] + hn_shapes,
        scratch_shapes=[pltpu.VMEM((tm, D_MODEL), BF16)],
        compiler_params=_params("parallel", "arbitrary"),
        name="conv_epilogue",
    )(proj, proj, proj, proj, proj, proj, x2, conv_w, wout,
      gpost.reshape(1, -1), gnext.reshape(1, -1), *perms)


def _attention_layer(x2, hns, w_in, w_out, gpost, gnext, batch, seq, emit):
    t = x2.shape[0]
    n_qkv = 3 * N_GROUPS * ATTN_WIDTH
    n_pairs = N_HEADS // HEADS_PER_STEP
    pair = HEADS_PER_STEP * HEAD_DIM
    part_scale = jnp.asarray([Q_SCALE, 1.0, 1.0], F32).reshape(1, 3, 1, 1, 1)
    wqkv = (w_in[:, :n_qkv].reshape(D_MODEL, 3, N_GROUPS, n_pairs, pair) * part_scale).astype(BF16)
    w_groups = jnp.transpose(wqkv, (2, 0, 3, 1, 4)).reshape(N_GROUPS, D_MODEL, 3 * ATTN_WIDTH)
    os_, ms, ls = [], [], []
    for g, dil in enumerate(DILATIONS):
        o, m, l = _proj_attention(hns[g].reshape(t, D_MODEL), w_groups, g, dil, seq)
        os_.append(o if dil == 1 else o.reshape(batch, dil, seq // dil, ATTN_WIDTH))

        def to_token_order(a, dil=dil):
            a = a.reshape(batch, dil, seq // dil, HEAD_DIM)
            return jnp.swapaxes(a, 1, 2).reshape(t, HEAD_DIM)

        ms.append(to_token_order(m))
        ls.append(to_token_order(l))
    gate = _proj(hns[0], w_in[:, n_qkv:].astype(BF16), lambda j: j, ATTN_WIDTH // PROJ_COLS,
                 "proj_gate")
    return _attn_epilogue(os_, ms + ls, gate, x2, w_out.astype(BF16), gpost, gnext,
                          batch, seq, emit)


def _conv_layer(x2, hn, w_in, conv_w, w_out, gpost, gnext, batch, seq, emit):
    proj = _proj(hn, w_in.astype(BF16), lambda j: j, 4 * D_MODEL // PROJ_COLS, "proj_conv")
    return _conv_epilogue(proj, x2, conv_w, w_out.astype(BF16), gpost, gnext, batch, seq, emit)


def kernel(x, l0_norm_pre, l0_w_in, l0_w_out, l0_norm_post, l1_norm_pre, l1_w_in, l1_conv_w, l1_w_out, l1_norm_post, l2_norm_pre, l2_w_in, l2_w_out, l2_norm_post, l3_norm_pre, l3_w_in, l3_conv_w, l3_w_out, l3_norm_post):
    batch, seq, d = x.shape
    assert d == D_MODEL and seq % (DILATIONS[-1] * BLOCK) == 0
    t = batch * seq
    x2 = x.reshape(t, D_MODEL)
    hns = _prenorm(x2, l0_norm_pre, batch, seq, EPILOGUE_TOKENS)
    x2, hn = _attention_layer(x2, hns, l0_w_in, l0_w_out, l0_norm_post, l1_norm_pre, batch, seq, 1)
    x2, *hns = _conv_layer(x2, hn, l1_w_in, l1_conv_w, l1_w_out, l1_norm_post, l2_norm_pre,
                           batch, seq, 2)
    x2, hn = _attention_layer(x2, hns, l2_w_in, l2_w_out, l2_norm_post, l3_norm_pre, batch, seq, 1)
    (x2,) = _conv_layer(x2, hn, l3_w_in, l3_conv_w, l3_w_out, l3_norm_post, l3_norm_pre,
                        batch, seq, 0)
    return x2.reshape(batch, seq, D_MODEL)
```

```python
import functools

import jax
import jax.numpy as jnp
import numpy as np
from jax import lax
from jax.experimental import pallas as pl
from jax.experimental.pallas import tpu as pltpu

D_MODEL = 2048
HEAD_DIM = 128
N_HEADS = 16
ATTN_WIDTH = N_HEADS * HEAD_DIM
DILATIONS = (1, 4, 16)
N_GROUPS = len(DILATIONS)
BLOCK = 128
CONV_K = 3
NORM_EPS = 1e-6
HALO_ROWS = 8
V7X_VMEM_LIMIT_BYTES = 56 * 1024 * 1024
EPILOGUE_TOKENS = 256
PROJ_ROWS = 1024
PROJ_COLS = 1024
ATTN_ROWS = 1024
HEADS_PER_STEP = 2

LOG2E = 1.4426950408889634
Q_SCALE = HEAD_DIM ** -0.5 * LOG2E

BF16 = jnp.bfloat16
F32 = jnp.float32


def _alibi_slopes():
    n = N_GROUPS * N_HEADS
    s = 2.0 ** (-8.0 * np.arange(1, n + 1) / n)
    return s.reshape(N_GROUPS, N_HEADS).astype(np.float32)


def _to_token_order(tm, dil):
    p = np.zeros((tm, tm), np.float32)
    u, r = np.meshgrid(np.arange(tm // dil), np.arange(dil), indexing="ij")
    p[(u * dil + r).ravel(), (r * (tm // dil) + u).ravel()] = 1.0
    return p


def _params(*semantics):
    return pltpu.CompilerParams(dimension_semantics=semantics,
                                vmem_limit_bytes=V7X_VMEM_LIMIT_BYTES)


def _rms(v, g):
    return v * lax.rsqrt(jnp.mean(v * v, axis=-1, keepdims=True) + NORM_EPS) * g


def _silu(g):
    return g / (1.0 + jnp.exp(-g))


def _tok_spec(tm, width, tiles_per_seq, col=0):
    return pl.BlockSpec((tm, width), lambda b, i: (b * tiles_per_seq + i, col))


def _res_spec(tm, width, dil):
    return pl.BlockSpec((None, dil, tm // dil, width), lambda b, i: (b, 0, i, 0))


def _const_spec(shape):
    return pl.BlockSpec(shape, lambda b, i: (0,) * len(shape), pipeline_mode=pl.Buffered(1))


def _hn_out(batch, seq, tm, emit):
    t = batch * seq
    specs, shapes = [], []
    if emit >= 1:
        specs.append(_tok_spec(tm, D_MODEL, seq // tm))
        shapes.append(jax.ShapeDtypeStruct((t, D_MODEL), BF16))
    if emit >= 2:
        for dil in DILATIONS[1:]:
            specs.append(_res_spec(tm, D_MODEL, dil))
            shapes.append(jax.ShapeDtypeStruct((batch, dil, seq // dil, D_MODEL), BF16))
    return specs, shapes


def _perm_inputs(tm, emit_or_read):
    if not emit_or_read:
        return [], []
    mats = []
    for dil in DILATIONS[1:]:
        p = _to_token_order(tm, dil)
        mats += [p, p.T]
    return [jnp.asarray(m, BF16) for m in mats], [_const_spec((tm, tm))] * len(mats)


def _emit_hn(hn, perm_refs, out_refs):
    if not out_refs:
        return
    hb = hn.astype(BF16)
    out_refs[0][...] = hb
    for k, o_ref in enumerate(out_refs[1:]):
        to_res = perm_refs[2 * k + 1][...]
        res = jnp.dot(to_res, hb, preferred_element_type=F32).astype(BF16)
        dil, rows = o_ref.shape[0], o_ref.shape[1]
        for r in range(dil):
            o_ref[r] = res[r * rows:(r + 1) * rows]


def _norm_kernel(x_ref, g_ref, *refs):
    perm_refs, out_refs = refs[:4], refs[4:]
    _emit_hn(_rms(x_ref[...], g_ref[...]), perm_refs, out_refs)


def _prenorm(x2, g, batch, seq, tm):
    perms, perm_specs = _perm_inputs(tm, True)
    out_specs, out_shapes = _hn_out(batch, seq, tm, 2)
    return pl.pallas_call(
        _norm_kernel,
        grid=(batch, seq // tm),
        in_specs=[_tok_spec(tm, D_MODEL, seq // tm), _const_spec((1, D_MODEL))] + perm_specs,
        out_specs=out_specs,
        out_shape=out_shapes,
        compiler_params=_params("parallel", "parallel"),
        name="prenorm",
    )(x2, g.reshape(1, D_MODEL), *perms)


def _mm_kernel(a_ref, w_ref, o_ref):
    o_ref[...] = jnp.dot(a_ref[...], w_ref[...],
                         preferred_element_type=F32).astype(o_ref.dtype)


def _proj(hn, w, col_block, n_col_blocks, name):
    t = hn.shape[0]
    tm = min(PROJ_ROWS, t)
    return pl.pallas_call(
        _mm_kernel,
        grid=(t // tm, n_col_blocks),
        in_specs=[pl.BlockSpec((tm, D_MODEL), lambda i, j: (i, 0)),
                  pl.BlockSpec((D_MODEL, PROJ_COLS), lambda i, j: (0, col_block(j)))],
        out_specs=pl.BlockSpec((tm, PROJ_COLS), lambda i, j: (i, j)),
        out_shape=jax.ShapeDtypeStruct((t, n_col_blocks * PROJ_COLS), BF16),
        compiler_params=_params("parallel", "arbitrary"),
        name=name,
    )(hn, w)


def _attn_bias_table(g, dil):
    qi = np.arange(BLOCK)[:, None]
    kj = np.arange(2 * BLOCK)[None, :]
    dist = qi + BLOCK - kj
    in_window = (dist >= 0) & (dist <= BLOCK)
    slopes = _alibi_slopes()[g].astype(np.float64) * LOG2E
    bias = -(slopes[:, None, None] * (dil * dist)[None].astype(np.float64))
    general = np.where(in_window[None], bias, -np.inf)
    first = np.where((kj >= BLOCK)[None], general, -np.inf)
    return np.stack([first, general]).astype(np.float32)


def _attend(q, kp, kc, vp, vc, bias):
    k = jnp.concatenate([kp, kc], axis=0)
    v = jnp.concatenate([vp, vc], axis=0)
    s = lax.dot_general(q, k, (((1,), (1,)), ((), ())), preferred_element_type=F32) + bias
    m = jnp.max(s, axis=-1, keepdims=True)
    p = jnp.exp2(s - m)
    l = jnp.sum(p, axis=-1, keepdims=True)
    return jnp.dot(p.astype(BF16), v, preferred_element_type=F32), m, l


def _proj_attn_kernel(hn_ref, wq_ref, wk_ref, wv_ref, bias_ref, o_ref, m_ref, l_ref, slab_ref,
                      halo_ref, *, tiles_per_seq):
    i, hp = pl.program_id(0), pl.program_id(1)
    tr = hn_ref.shape[0]
    pair = HEADS_PER_STEP * HEAD_DIM

    @pl.when((i == 0) & (hp == 0))
    def _():
        slab_ref[...] = jnp.zeros_like(slab_ref)
        halo_ref[...] = jnp.zeros_like(halo_ref)

    @pl.when(hp == 0)
    def _():
        m_ref[...] = jnp.zeros_like(m_ref)
        l_ref[...] = jnp.ones_like(l_ref)

    first = ((i - 1) % tiles_per_seq) == 0
    lane = lax.broadcasted_iota(jnp.int32, (BLOCK, HEAD_DIM), 1)

    def cols(part, hh):
        return slice(part * pair + hh * HEAD_DIM, part * pair + (hh + 1) * HEAD_DIM)

    for qb in range(tr // BLOCK):
        rows = slice(qb * BLOCK, (qb + 1) * BLOCK)
        prows = slice((qb - 1) * BLOCK, qb * BLOCK)
        m_t, l_t = m_ref[rows, :], l_ref[rows, :]
        for hh in range(HEADS_PER_STEP):
            if qb == 0:
                kp, vp = halo_ref[hp, :, cols(0, hh)], halo_ref[hp, :, cols(1, hh)]
                bias = jnp.where(first, bias_ref[0, hh], bias_ref[1, hh])
            else:
                kp, vp = slab_ref[hp, prows, cols(1, hh)], slab_ref[hp, prows, cols(2, hh)]
                bias = bias_ref[1, hh]
            o, m, l = _attend(slab_ref[hp, rows, cols(0, hh)], kp, slab_ref[hp, rows, cols(1, hh)],
                              vp, slab_ref[hp, rows, cols(2, hh)], bias)
            o_ref[rows, hh * HEAD_DIM:(hh + 1) * HEAD_DIM] = o.astype(o_ref.dtype)
            head = hp * HEADS_PER_STEP + hh
            m_t = jnp.where(lane == head, m, m_t)
            l_t = jnp.where(lane == head, l, l_t)
        m_ref[rows, :] = m_t
        l_ref[rows, :] = l_t
    halo_ref[hp] = slab_ref[hp, tr - BLOCK:tr, pair:3 * pair]
    hn = hn_ref[...]
    for part, w_ref in enumerate((wq_ref, wk_ref, wv_ref)):
        slab_ref[hp, :, part * pair:(part + 1) * pair] = jnp.dot(
            hn, w_ref[...], preferred_element_type=F32).astype(slab_ref.dtype)


def _proj_attention(hn, w, g, dil, seq):
    t = hn.shape[0]
    l_sub = seq // dil
    tr = min(ATTN_ROWS, l_sub)
    n_tiles = t // tr
    pair = HEADS_PER_STEP * HEAD_DIM
    n_pairs = N_HEADS // HEADS_PER_STEP
    bias = jnp.asarray(_attn_bias_table(g, dil))

    def tile_done(i, hp):
        return (jnp.maximum(i - 1, 0), 0)

    def w_spec(part):
        first = (part * N_GROUPS + g) * n_pairs
        return pl.BlockSpec((D_MODEL, pair), lambda i, hp: (0, first + hp))

    kern = functools.partial(_proj_attn_kernel, tiles_per_seq=l_sub // tr)
    return pl.pallas_call(
        kern,
        grid=(n_tiles + 1, n_pairs),
        in_specs=[pl.BlockSpec((tr, D_MODEL), lambda i, hp: (jnp.minimum(i, n_tiles - 1), 0)),
                  w_spec(0), w_spec(1), w_spec(2),
                  pl.BlockSpec((2, HEADS_PER_STEP, BLOCK, 2 * BLOCK),
                               lambda i, hp: (0, hp, 0, 0))],
        out_specs=[pl.BlockSpec((tr, pair),
                                lambda i, hp: (jnp.maximum(i - 1, 0), jnp.where(i == 0, 0, hp))),
                   pl.BlockSpec((tr, HEAD_DIM), tile_done),
                   pl.BlockSpec((tr, HEAD_DIM), tile_done)],
        out_shape=[jax.ShapeDtypeStruct((t, ATTN_WIDTH), BF16),
                   jax.ShapeDtypeStruct((t, HEAD_DIM), F32),
                   jax.ShapeDtypeStruct((t, HEAD_DIM), F32)],
        scratch_shapes=[pltpu.VMEM((n_pairs, tr, 3 * pair), BF16),
                        pltpu.VMEM((n_pairs, BLOCK, 2 * pair), BF16)],
        compiler_params=_params("arbitrary", "arbitrary"),
        name=f"proj_attn{g}",
    )(hn, w, w, w, bias)


def _finish(y_ref, x_ref, wout_ref, gpost_ref, gnext_ref, perm_refs, xo_ref, hn_refs):
    out = jnp.dot(y_ref[...], wout_ref[...], preferred_element_type=F32)
    xn = x_ref[...] + _rms(out, gpost_ref[...])
    xo_ref[...] = xn
    if hn_refs:
        _emit_hn(_rms(xn, gnext_ref[...]), perm_refs, hn_refs)


def _attn_out_kernel(o0_ref, o1_ref, o2_ref, m0_ref, m1_ref, m2_ref, l0_ref, l1_ref, l2_ref,
                     gate_ref, x_ref, wout_ref, gpost_ref, gnext_ref, *refs):
    perm_refs, xo_ref, hn_refs, y_ref = refs[:4], refs[4], refs[5:-1], refs[-1]
    tm = x_ref.shape[0]
    ms = [m0_ref[...], m1_ref[...], m2_ref[...]]
    m = jnp.maximum(jnp.maximum(ms[0], ms[1]), ms[2])
    es = [jnp.exp2(t - m) for t in ms]
    den = es[0] * l0_ref[...] + es[1] * l1_ref[...] + es[2] * l2_ref[...]
    ws = [e / den for e in es]
    os_ = [o0_ref[...].astype(F32)]
    for k, o_ref in enumerate((o1_ref, o2_ref)):
        res = o_ref[...].reshape(tm, ATTN_WIDTH)
        os_.append(jnp.dot(perm_refs[2 * k][...], res, preferred_element_type=F32))
    for h in range(N_HEADS):
        hs = slice(h * HEAD_DIM, (h + 1) * HEAD_DIM)
        acc = ws[0][:, h:h + 1] * os_[0][:, hs]
        for g in range(1, N_GROUPS):
            acc = acc + ws[g][:, h:h + 1] * os_[g][:, hs]
        y_ref[:, hs] = (acc * _silu(gate_ref[:, hs].astype(F32))).astype(y_ref.dtype)
    _finish(y_ref, x_ref, wout_ref, gpost_ref, gnext_ref, perm_refs, xo_ref, hn_refs)


def _attn_epilogue(os_, stats, gate, x2, wout, gpost, gnext, batch, seq, emit):
    tm = EPILOGUE_TOKENS
    tps = seq // tm
    t = batch * seq
    perms, perm_specs = _perm_inputs(tm, True)
    hn_specs, hn_shapes = _hn_out(batch, seq, tm, emit)
    return pl.pallas_call(
        _attn_out_kernel,
        grid=(batch, tps),
        in_specs=[_tok_spec(tm, ATTN_WIDTH, tps), _res_spec(tm, ATTN_WIDTH, DILATIONS[1]),
                  _res_spec(tm, ATTN_WIDTH, DILATIONS[2])]
                 + [_tok_spec(tm, HEAD_DIM, tps)] * (2 * N_GROUPS)
                 + [_tok_spec(tm, ATTN_WIDTH, tps), _tok_spec(tm, D_MODEL, tps),
                    _const_spec((ATTN_WIDTH, D_MODEL)), _const_spec((1, D_MODEL)),
                    _const_spec((1, D_MODEL))] + perm_specs,
        out_specs=[_tok_spec(tm, D_MODEL, tps)] + hn_specs,
        out_shape=[jax.ShapeDtypeStruct((t, D_MODEL), F32)] + hn_shapes,
        scratch_shapes=[pltpu.VMEM((tm, ATTN_WIDTH), BF16)],
        compiler_params=_params("parallel", "parallel"),
        name="attn_epilogue",
    )(*os_, *stats, gate, x2, wout, gpost.reshape(1, -1), gnext.reshape(1, -1), *perms)


def _conv_out_kernel(h_ref, b_ref, c_ref, gate_ref, hh_ref, ch_ref, x_ref, cw_ref,
                     wout_ref, gpost_ref, gnext_ref, *refs, n_perm):
    perm_refs, xo_ref = refs[:n_perm], refs[n_perm]
    hn_refs, y_ref = refs[n_perm + 1:-1], refs[-1]
    tm = h_ref.shape[0]
    first = pl.program_id(1) == 0
    u = c_ref[...].astype(F32) * h_ref[...].astype(F32)
    uh = ch_ref[...].astype(F32) * hh_ref[...].astype(F32)
    uh = jnp.where(first, 0.0, uh)
    ue = jnp.concatenate([uh, u], axis=0)
    um1 = ue[HALO_ROWS - 1:HALO_ROWS - 1 + tm]
    um2 = ue[HALO_ROWS - 2:HALO_ROWS - 2 + tm]
    cw = cw_ref[...]
    conv = cw[0:1] * um2 + cw[1:2] * um1 + cw[2:3] * u
    y = b_ref[...].astype(F32) * conv * _silu(gate_ref[...].astype(F32))
    y_ref[...] = y.astype(y_ref.dtype)
    _finish(y_ref, x_ref, wout_ref, gpost_ref, gnext_ref, perm_refs, xo_ref, hn_refs)


def _conv_epilogue(proj, x2, conv_w, wout, gpost, gnext, batch, seq, emit):
    tm = EPILOGUE_TOKENS
    tps = seq // tm
    t = batch * seq
    hb = tm // HALO_ROWS

    def halo(col):
        return pl.BlockSpec((HALO_ROWS, D_MODEL),
                            lambda b, i: (jnp.maximum((b * tps + i) * hb - 1, 0), col))

    perms, perm_specs = _perm_inputs(tm, emit >= 2)
    hn_specs, hn_shapes = _hn_out(batch, seq, tm, emit)
    kern = functools.partial(_conv_out_kernel, n_perm=len(perms))
    return pl.pallas_call(
        kern,
        grid=(batch, tps),
        in_specs=[_tok_spec(tm, D_MODEL, tps, c) for c in range(4)]
                 + [halo(0), halo(2), _tok_spec(tm, D_MODEL, tps),
                    _const_spec((CONV_K, D_MODEL)), _const_spec((D_MODEL, D_MODEL)),
                    _const_spec((1, D_MODEL)), _const_spec((1, D_MODEL))] + perm_specs,
        out_specs=[_tok_spec(tm, D_MODEL, tps)] + hn_specs,
        out_shape=[jax.ShapeDtypeStruct((t, D_MODEL), F32)] + hn_shapes,
        scratch_shapes=[pltpu.VMEM((tm, D_MODEL), BF16)],
        compiler_params=_params("parallel", "arbitrary"),
        name="conv_epilogue",
    )(proj, proj, proj, proj, proj, proj, x2, conv_w, wout,
      gpost.reshape(1, -1), gnext.reshape(1, -1), *perms)


def _attention_layer(x2, hns, w_in, w_out, gpost, gnext, batch, seq, emit):
    t = x2.shape[0]
    n_qkv = 3 * N_GROUPS * ATTN_WIDTH
    col_scale = jnp.where(jnp.arange(w_in.shape[1]) < N_GROUPS * ATTN_WIDTH, Q_SCALE, 1.0)
    w = (w_in * col_scale.astype(F32)[None, :]).astype(BF16)
    os_, ms, ls = [], [], []
    for g, dil in enumerate(DILATIONS):
        o, m, l = _proj_attention(hns[g].reshape(t, D_MODEL), w, g, dil, seq)
        os_.append(o if dil == 1 else o.reshape(batch, dil, seq // dil, ATTN_WIDTH))

        def to_token_order(a, dil=dil):
            a = a.reshape(batch, dil, seq // dil, HEAD_DIM)
            return jnp.swapaxes(a, 1, 2).reshape(t, HEAD_DIM)

        ms.append(to_token_order(m))
        ls.append(to_token_order(l))
    gate0 = n_qkv // PROJ_COLS
    gate = _proj(hns[0], w, lambda j: gate0 + j, ATTN_WIDTH // PROJ_COLS, "proj_gate")
    return _attn_epilogue(os_, ms + ls, gate, x2, w_out.astype(BF16), gpost, gnext,
                          batch, seq, emit)


def _conv_layer(x2, hn, w_in, conv_w, w_out, gpost, gnext, batch, seq, emit):
    proj = _proj(hn, w_in.astype(BF16), lambda j: j, 4 * D_MODEL // PROJ_COLS, "proj_conv")
    return _conv_epilogue(proj, x2, conv_w, w_out.astype(BF16), gpost, gnext, batch, seq, emit)


def kernel(x, l0_norm_pre, l0_w_in, l0_w_out, l0_norm_post, l1_norm_pre, l1_w_in, l1_conv_w, l1_w_out, l1_norm_post, l2_norm_pre, l2_w_in, l2_w_out, l2_norm_post, l3_norm_pre, l3_w_in, l3_conv_w, l3_w_out, l3_norm_post):
    batch, seq, d = x.shape
    assert d == D_MODEL and seq % (DILATIONS[-1] * BLOCK) == 0
    t = batch * seq
    x2 = x.reshape(t, D_MODEL)
    hns = _prenorm(x2, l0_norm_pre, batch, seq, EPILOGUE_TOKENS)
    x2, hn = _attention_layer(x2, hns, l0_w_in, l0_w_out, l0_norm_post, l1_norm_pre, batch, seq, 1)
    x2, *hns = _conv_layer(x2, hn, l1_w_in, l1_conv_w, l1_w_out, l1_norm_post, l2_norm_pre,
                           batch, seq, 2)
    x2, hn = _attention_layer(x2, hns, l2_w_in, l2_w_out, l2_norm_post, l3_norm_pre, batch, seq, 1)
    (x2,) = _conv_layer(x2, hn, l3_w_in, l3_conv_w, l3_w_out, l3_norm_post, l3_norm_pre,
                        batch, seq, 0)
    return x2.reshape(batch, seq, D_MODEL)
```

```python
import functools

import jax
import jax.numpy as jnp
import numpy as np
from jax import lax
from jax.experimental import pallas as pl
from jax.experimental.pallas import tpu as pltpu

D_MODEL = 2048
HEAD_DIM = 128
N_HEADS = 16
ATTN_WIDTH = N_HEADS * HEAD_DIM
DILATIONS = (1, 4, 16)
N_GROUPS = len(DILATIONS)
BLOCK = 128
CONV_K = 3
NORM_EPS = 1e-6
HALO_ROWS = 8
V7X_VMEM_LIMIT_BYTES = 56 * 1024 * 1024
EPILOGUE_TOKENS = 256
PROJ_ROWS = 1024
PROJ_COLS = 1024
CONV_COLS = 256
ATTN_ROWS = 1024
HEADS_PER_STEP = 2
LOG2E = 1.4426950408889634
Q_SCALE = HEAD_DIM ** -0.5 * LOG2E

BF16 = jnp.bfloat16
F32 = jnp.float32


def _alibi_slopes():
    n = N_GROUPS * N_HEADS
    s = 2.0 ** (-8.0 * np.arange(1, n + 1) / n)
    return s.reshape(N_GROUPS, N_HEADS).astype(np.float32)


def _to_token_order(tm, dil):
    p = np.zeros((tm, tm), np.float32)
    u, r = np.meshgrid(np.arange(tm // dil), np.arange(dil), indexing="ij")
    p[(u * dil + r).ravel(), (r * (tm // dil) + u).ravel()] = 1.0
    return p


def _params(*semantics):
    return pltpu.CompilerParams(dimension_semantics=semantics,
                                vmem_limit_bytes=V7X_VMEM_LIMIT_BYTES)


def _rms(v, g):
    return v * lax.rsqrt(jnp.mean(v * v, axis=-1, keepdims=True) + NORM_EPS) * g


def _silu(g):
    half = 0.5 * g
    return half + half * jnp.tanh(half)


class _Tiles:
    def __init__(self, batch, seq, tm):
        self.batch, self.seq, self.tm = batch, seq, tm
        self.per_seq = seq // tm
        self.n = batch * self.per_seq

    def tile(self, s, lag):
        return jnp.clip(s - lag, 0, self.n - 1)

    def tok(self, width, lag, col=0):
        return pl.BlockSpec((self.tm, width), lambda s: (self.tile(s, lag), col))

    def res(self, width, dil, lag):
        def index(s):
            j = self.tile(s, lag)
            return (j // self.per_seq, 0, j % self.per_seq, 0)
        return pl.BlockSpec((None, dil, self.tm // dil, width), index)

    def hn_out(self, emit, lag):
        specs, shapes = [], []
        if emit >= 1:
            specs.append(self.tok(D_MODEL, lag))
            shapes.append(jax.ShapeDtypeStruct((self.batch * self.seq, D_MODEL), BF16))
        if emit >= 2:
            for dil in DILATIONS[1:]:
                specs.append(self.res(D_MODEL, dil, lag))
                shapes.append(jax.ShapeDtypeStruct(
                    (self.batch, dil, self.seq // dil, D_MODEL), BF16))
        return specs, shapes


def _const_spec(shape, block_index=None):
    index = block_index or (0,) * len(shape)
    return pl.BlockSpec(shape, lambda s: index, pipeline_mode=pl.Buffered(1))


def _perm_inputs(tm, wanted):
    if not wanted:
        return [], []
    mats = []
    for dil in DILATIONS[1:]:
        p = _to_token_order(tm, dil)
        mats += [p, p.T]
    return [jnp.asarray(m, BF16) for m in mats], [_const_spec((tm, tm))] * len(mats)


def _emit_hn(hn, perm_refs, out_refs):
    if not out_refs:
        return
    hb = hn.astype(BF16)
    out_refs[0][...] = hb
    for k, o_ref in enumerate(out_refs[1:]):
        to_res = perm_refs[2 * k + 1][...]
        res = jnp.dot(to_res, hb, preferred_element_type=F32).astype(BF16)
        dil, rows = o_ref.shape[0], o_ref.shape[1]
        for r in range(dil):
            o_ref[r] = res[r * rows:(r + 1) * rows]


def _norm_kernel(x_ref, g_ref, *refs):
    perm_refs, out_refs = refs[:4], refs[4:]
    _emit_hn(_rms(x_ref[...], g_ref[...]), perm_refs, out_refs)


def _prenorm(x2, g, tiles):
    perms, perm_specs = _perm_inputs(tiles.tm, True)
    out_specs, out_shapes = tiles.hn_out(2, 0)
    return pl.pallas_call(
        _norm_kernel,
        grid=(tiles.n,),
        in_specs=[tiles.tok(D_MODEL, 0), _const_spec((1, D_MODEL))] + perm_specs,
        out_specs=out_specs,
        out_shape=out_shapes,
        compiler_params=_params("parallel"),
        name="prenorm",
    )(x2, g.reshape(1, D_MODEL), *perms)


def _mm_kernel(a_ref, w_ref, o_ref):
    o_ref[...] = jnp.dot(a_ref[...], w_ref[...],
                         preferred_element_type=F32).astype(o_ref.dtype)


def _proj(hn, w, name):
    t, n = hn.shape[0], w.shape[1]
    tm = min(PROJ_ROWS, t)
    return pl.pallas_call(
        _mm_kernel,
        grid=(t // tm, n // PROJ_COLS),
        in_specs=[pl.BlockSpec((tm, D_MODEL), lambda i, j: (i, 0)),
                  pl.BlockSpec((D_MODEL, PROJ_COLS), lambda i, j: (0, j))],
        out_specs=pl.BlockSpec((tm, PROJ_COLS), lambda i, j: (i, j)),
        out_shape=jax.ShapeDtypeStruct((t, n), BF16),
        compiler_params=_params("parallel", "arbitrary"),
        name=name,
    )(hn, w)


def _attn_bias_table(g, dil):
    qi = np.arange(BLOCK)[:, None]
    kj = np.arange(2 * BLOCK)[None, :]
    dist = qi + BLOCK - kj
    in_window = (dist >= 0) & (dist <= BLOCK)
    slopes = _alibi_slopes()[g].astype(np.float64) * LOG2E
    bias = -(slopes[:, None, None] * (dil * dist)[None].astype(np.float64))
    general = np.where(in_window[None], bias, -np.inf)
    first = np.where((kj >= BLOCK)[None], general, -np.inf)
    return np.stack([first, general]).astype(np.float32)


def _attend(q, kp, kc, vp, vc, bias):
    k = jnp.concatenate([kp, kc], axis=0)
    v = jnp.concatenate([vp, vc], axis=0)
    s = lax.dot_general(q, k, (((1,), (1,)), ((), ())), preferred_element_type=F32) + bias
    m = jnp.max(s, axis=-1, keepdims=True)
    p = jnp.exp2(s - m)
    l = jnp.sum(p, axis=-1, keepdims=True)
    return jnp.dot(p.astype(BF16), v, preferred_element_type=F32), m, l


def _proj_attn_kernel(hn_ref, wq_ref, wk_ref, wv_ref, bias_ref, o_ref, m_ref, l_ref, slab_ref,
                      halo_ref, *, tiles_per_seq):
    i, hp = pl.program_id(0), pl.program_id(1)
    tr = hn_ref.shape[0]
    pair = HEADS_PER_STEP * HEAD_DIM

    @pl.when((i == 0) & (hp == 0))
    def _():
        slab_ref[...] = jnp.zeros_like(slab_ref)
        halo_ref[...] = jnp.zeros_like(halo_ref)

    @pl.when(hp == 0)
    def _():
        m_ref[...] = jnp.zeros_like(m_ref)
        l_ref[...] = jnp.ones_like(l_ref)

    first = ((i - 1) % tiles_per_seq) == 0
    lane = lax.broadcasted_iota(jnp.int32, (BLOCK, HEAD_DIM), 1)

    def cols(part, hh):
        return slice(part * pair + hh * HEAD_DIM, part * pair + (hh + 1) * HEAD_DIM)

    for qb in range(tr // BLOCK):
        rows = slice(qb * BLOCK, (qb + 1) * BLOCK)
        prows = slice((qb - 1) * BLOCK, qb * BLOCK)
        m_t, l_t = m_ref[rows, :], l_ref[rows, :]
        for hh in range(HEADS_PER_STEP):
            if qb == 0:
                kp, vp = halo_ref[hp, :, cols(0, hh)], halo_ref[hp, :, cols(1, hh)]
                bias = jnp.where(first, bias_ref[0, hh], bias_ref[1, hh])
            else:
                kp, vp = slab_ref[hp, prows, cols(1, hh)], slab_ref[hp, prows, cols(2, hh)]
                bias = bias_ref[1, hh]
            o, m, l = _attend(slab_ref[hp, rows, cols(0, hh)], kp, slab_ref[hp, rows, cols(1, hh)],
                              vp, slab_ref[hp, rows, cols(2, hh)], bias)
            o_ref[rows, hh * HEAD_DIM:(hh + 1) * HEAD_DIM] = o.astype(o_ref.dtype)
            head = hp * HEADS_PER_STEP + hh
            m_t = jnp.where(lane == head, m, m_t)
            l_t = jnp.where(lane == head, l, l_t)
        m_ref[rows, :] = m_t
        l_ref[rows, :] = l_t
    halo_ref[hp] = slab_ref[hp, tr - BLOCK:tr, pair:3 * pair]
    hn = hn_ref[...]
    for part, w_ref in enumerate((wq_ref, wk_ref, wv_ref)):
        slab_ref[hp, :, part * pair:(part + 1) * pair] = jnp.dot(
            hn, w_ref[...], preferred_element_type=F32).astype(slab_ref.dtype)


def _proj_attention(hn, w, g, dil, seq):
    t = hn.shape[0]
    l_sub = seq // dil
    tr = min(ATTN_ROWS, l_sub)
    n_tiles = t // tr
    pair = HEADS_PER_STEP * HEAD_DIM
    n_pairs = N_HEADS // HEADS_PER_STEP
    bias = jnp.asarray(_attn_bias_table(g, dil))

    def tile_done(i, hp):
        return (jnp.maximum(i - 1, 0), 0)

    def w_spec(part):
        first = (part * N_GROUPS + g) * n_pairs
        return pl.BlockSpec((D_MODEL, pair), lambda i, hp: (0, first + hp))

    kern = functools.partial(_proj_attn_kernel, tiles_per_seq=l_sub // tr)
    return pl.pallas_call(
        kern,
        grid=(n_tiles + 1, n_pairs),
        in_specs=[pl.BlockSpec((tr, D_MODEL), lambda i, hp: (jnp.minimum(i, n_tiles - 1), 0)),
                  w_spec(0), w_spec(1), w_spec(2),
                  pl.BlockSpec((2, HEADS_PER_STEP, BLOCK, 2 * BLOCK),
                               lambda i, hp: (0, hp, 0, 0))],
        out_specs=[pl.BlockSpec((tr, pair),
                                lambda i, hp: (jnp.maximum(i - 1, 0), jnp.where(i == 0, 0, hp))),
                   pl.BlockSpec((tr, HEAD_DIM), tile_done),
                   pl.BlockSpec((tr, HEAD_DIM), tile_done)],
        out_shape=[jax.ShapeDtypeStruct((t, ATTN_WIDTH), BF16),
                   jax.ShapeDtypeStruct((t, HEAD_DIM), F32),
                   jax.ShapeDtypeStruct((t, HEAD_DIM), F32)],
        scratch_shapes=[pltpu.VMEM((n_pairs, tr, 3 * pair), BF16),
                        pltpu.VMEM((n_pairs, BLOCK, 2 * pair), BF16)],
        compiler_params=_params("arbitrary", "arbitrary"),
        name=f"proj_attn{g}",
    )(hn, w, w, w, bias)


def _finish(y_ref, x_ref, wout_ref, gpost_ref, gnext_ref, perm_refs, xo_ref, hn_refs):
    out = jnp.dot(y_ref[...], wout_ref[...], preferred_element_type=F32)
    xn = x_ref[...] + _rms(out, gpost_ref[...])
    xo_ref[...] = xn
    if hn_refs:
        _emit_hn(_rms(xn, gnext_ref[...]), perm_refs, hn_refs)


def _zero_on_first_step(y_ref):
    @pl.when(pl.program_id(0) == 0)
    def _():
        y_ref[...] = jnp.zeros_like(y_ref)


def _attn_out_kernel(o0_ref, o1_ref, o2_ref, m0_ref, m1_ref, m2_ref, l0_ref, l1_ref, l2_ref,
                     hn_ref, x_ref, wgate_ref, wout_ref, gpost_ref, gnext_ref, *refs):
    perm_refs, xo_ref, hn_refs, y_ref = refs[:4], refs[4], refs[5:-1], refs[-1]
    tm = x_ref.shape[0]
    _zero_on_first_step(y_ref)
    _finish(y_ref, x_ref, wout_ref, gpost_ref, gnext_ref, perm_refs, xo_ref, hn_refs)
    gate = jnp.dot(hn_ref[...], wgate_ref[...], preferred_element_type=F32)
    ms = [m0_ref[...], m1_ref[...], m2_ref[...]]
    m = jnp.maximum(jnp.maximum(ms[0], ms[1]), ms[2])
    es = [jnp.exp2(t - m) for t in ms]
    den = es[0] * l0_ref[...] + es[1] * l1_ref[...] + es[2] * l2_ref[...]
    ws = [e / den for e in es]
    os_ = [o0_ref[...].astype(F32)]
    for k, o_ref in enumerate((o1_ref, o2_ref)):
        res = o_ref[...].reshape(tm, ATTN_WIDTH)
        os_.append(jnp.dot(perm_refs[2 * k][...], res, preferred_element_type=F32))
    for h in range(N_HEADS):
        hs = slice(h * HEAD_DIM, (h + 1) * HEAD_DIM)
        acc = ws[0][:, h:h + 1] * os_[0][:, hs]
        for g in range(1, N_GROUPS):
            acc = acc + ws[g][:, h:h + 1] * os_[g][:, hs]
        y_ref[:, hs] = (acc * _silu(gate[:, hs])).astype(y_ref.dtype)


def _attn_epilogue(os_, stats, hn, x2, w, wout, gpost, gnext, tiles, emit):
    tm = tiles.tm
    perms, perm_specs = _perm_inputs(tm, True)
    hn_specs, hn_shapes = tiles.hn_out(emit, 1)
    gate_block = 3 * N_GROUPS
    return pl.pallas_call(
        _attn_out_kernel,
        grid=(tiles.n + 1,),
        in_specs=[tiles.tok(ATTN_WIDTH, 0), tiles.res(ATTN_WIDTH, DILATIONS[1], 0),
                  tiles.res(ATTN_WIDTH, DILATIONS[2], 0)]
                 + [tiles.tok(HEAD_DIM, 0)] * (2 * N_GROUPS)
                 + [tiles.tok(D_MODEL, 0), tiles.tok(D_MODEL, 1),
                    _const_spec((D_MODEL, ATTN_WIDTH), (0, gate_block)),
                    _const_spec((ATTN_WIDTH, D_MODEL)), _const_spec((1, D_MODEL)),
                    _const_spec((1, D_MODEL))] + perm_specs,
        out_specs=[tiles.tok(D_MODEL, 1)] + hn_specs,
        out_shape=[jax.ShapeDtypeStruct(x2.shape, F32)] + hn_shapes,
        scratch_shapes=[pltpu.VMEM((tm, ATTN_WIDTH), BF16)],
        compiler_params=_params("arbitrary"),
        name="attn_epilogue",
    )(*os_, *stats, hn, x2, w, wout, gpost.reshape(1, -1), gnext.reshape(1, -1), *perms)


def _conv_mix_kernel(hn_ref, wh_ref, wb_ref, wc_ref, wg_ref, cw_ref, y_ref, halo_ref, *,
                     tiles_per_seq):
    i, j = pl.program_id(0), pl.program_id(1)
    tr = hn_ref.shape[0]

    @pl.when((i == 0) & (j == 0))
    def _():
        halo_ref[...] = jnp.zeros_like(halo_ref)

    hn = hn_ref[...]
    u = (jnp.dot(hn, wc_ref[...], preferred_element_type=F32)
         * jnp.dot(hn, wh_ref[...], preferred_element_type=F32))
    first = (i % tiles_per_seq) == 0
    ue = jnp.concatenate([jnp.where(first, 0.0, halo_ref[j]), u], axis=0)
    um1 = ue[HALO_ROWS - 1:HALO_ROWS - 1 + tr]
    um2 = ue[HALO_ROWS - 2:HALO_ROWS - 2 + tr]
    halo_ref[j] = u[tr - HALO_ROWS:tr]
    cw = cw_ref[...]
    conv = cw[0:1] * um2 + cw[1:2] * um1 + cw[2:3] * u
    y = (jnp.dot(hn, wb_ref[...], preferred_element_type=F32) * conv
         * _silu(jnp.dot(hn, wg_ref[...], preferred_element_type=F32)))
    y_ref[...] = y.astype(y_ref.dtype)


def _conv_mix(hn, w, conv_w, seq):
    t = hn.shape[0]
    tr = min(PROJ_ROWS, seq)
    n_chunks = D_MODEL // CONV_COLS

    def w_spec(part):
        return pl.BlockSpec((D_MODEL, CONV_COLS), lambda i, j: (0, part * n_chunks + j))

    kern = functools.partial(_conv_mix_kernel, tiles_per_seq=seq // tr)
    return pl.pallas_call(
        kern,
        grid=(t // tr, n_chunks),
        in_specs=[pl.BlockSpec((tr, D_MODEL), lambda i, j: (i, 0)),
                  w_spec(0), w_spec(1), w_spec(2), w_spec(3),
                  pl.BlockSpec((CONV_K, CONV_COLS), lambda i, j: (0, j))],
        out_specs=pl.BlockSpec((tr, CONV_COLS), lambda i, j: (i, j)),
        out_shape=jax.ShapeDtypeStruct((t, D_MODEL), BF16),
        scratch_shapes=[pltpu.VMEM((n_chunks, HALO_ROWS, CONV_COLS), F32)],
        compiler_params=_params("arbitrary", "arbitrary"),
        name="conv_mix",
    )(hn, w, w, w, w, conv_w)


def _out_kernel(y_ref, x_ref, wout_ref, gpost_ref, gnext_ref, *refs, n_perm):
    perm_refs, xo_ref, hn_refs = refs[:n_perm], refs[n_perm], refs[n_perm + 1:]
    _finish(y_ref, x_ref, wout_ref, gpost_ref, gnext_ref, perm_refs, xo_ref, hn_refs)


def _out_epilogue(y, x2, wout, gpost, gnext, tiles, emit):
    perms, perm_specs = _perm_inputs(tiles.tm, emit >= 2)
    hn_specs, hn_shapes = tiles.hn_out(emit, 0)
    kern = functools.partial(_out_kernel, n_perm=len(perms))
    return pl.pallas_call(
        kern,
        grid=(tiles.n,),
        in_specs=[tiles.tok(D_MODEL, 0), tiles.tok(D_MODEL, 0),
                  _const_spec((D_MODEL, D_MODEL)), _const_spec((1, D_MODEL)),
                  _const_spec((1, D_MODEL))] + perm_specs,
        out_specs=[tiles.tok(D_MODEL, 0)] + hn_specs,
        out_shape=[jax.ShapeDtypeStruct(x2.shape, F32)] + hn_shapes,
        compiler_params=_params("parallel"),
        name="out_epilogue",
    )(y, x2, wout, gpost.reshape(1, -1), gnext.reshape(1, -1), *perms)


def _attention_layer(x2, hns, w_in, w_out, gpost, gnext, tiles, emit):
    t = x2.shape[0]
    batch, seq = tiles.batch, tiles.seq
    col_scale = jnp.where(jnp.arange(w_in.shape[1]) < N_GROUPS * ATTN_WIDTH, Q_SCALE, 1.0)
    w = (w_in * col_scale.astype(F32)[None, :]).astype(BF16)
    os_, ms, ls = [], [], []
    for g, dil in enumerate(DILATIONS):
        o, m, l = _proj_attention(hns[g].reshape(t, D_MODEL), w, g, dil, seq)
        os_.append(o if dil == 1 else o.reshape(batch, dil, seq // dil, ATTN_WIDTH))

        def to_token_order(a, dil=dil):
            a = a.reshape(batch, dil, seq // dil, HEAD_DIM)
            return jnp.swapaxes(a, 1, 2).reshape(t, HEAD_DIM)

        ms.append(to_token_order(m))
        ls.append(to_token_order(l))
    return _attn_epilogue(os_, ms + ls, hns[0], x2, w, w_out.astype(BF16), gpost, gnext,
                          tiles, emit)


def _conv_layer(x2, hn, w_in, conv_w, w_out, gpost, gnext, tiles, emit):
    y = _conv_mix(hn, w_in.astype(BF16), conv_w, tiles.seq)
    return _out_epilogue(y, x2, w_out.astype(BF16), gpost, gnext, tiles, emit)


def kernel(x, l0_norm_pre, l0_w_in, l0_w_out, l0_norm_post, l1_norm_pre, l1_w_in, l1_conv_w, l1_w_out, l1_norm_post, l2_norm_pre, l2_w_in, l2_w_out, l2_norm_post, l3_norm_pre, l3_w_in, l3_conv_w, l3_w_out, l3_norm_post):
    batch, seq, d = x.shape
    assert d == D_MODEL and seq % (DILATIONS[-1] * BLOCK) == 0
    tiles = _Tiles(batch, seq, EPILOGUE_TOKENS)
    x2 = x.reshape(batch * seq, D_MODEL)
    hns = _prenorm(x2, l0_norm_pre, tiles)
    x2, hn = _attention_layer(x2, hns, l0_w_in, l0_w_out, l0_norm_post, l1_norm_pre, tiles, 1)
    x2, *hns = _conv_layer(x2, hn, l1_w_in, l1_conv_w, l1_w_out, l1_norm_post, l2_norm_pre, tiles, 2)
    x2, hn = _attention_layer(x2, hns, l2_w_in, l2_w_out, l2_norm_post, l3_norm_pre, tiles, 1)
    (x2,) = _conv_layer(x2, hn, l3_w_in, l3_conv_w, l3_w_out, l3_norm_post, l3_norm_pre, tiles, 0)
    return x2.reshape(batch, seq, D_MODEL)
```

```python
import functools

import jax
import jax.numpy as jnp
import numpy as np
from jax import lax
from jax.experimental import pallas as pl
from jax.experimental.pallas import tpu as pltpu

D_MODEL = 2048
HEAD_DIM = 128
N_HEADS = 16
ATTN_WIDTH = N_HEADS * HEAD_DIM
DILATIONS = (1, 4, 16)
N_GROUPS = len(DILATIONS)
BLOCK = 128
CONV_K = 3
NORM_EPS = 1e-6
HALO_ROWS = 8
V7X_VMEM_LIMIT_BYTES = 56 * 1024 * 1024
EPILOGUE_TOKENS = 256
PROJ_ROWS = 1024
CONV_COLS = 256
ATTN_ROWS = 1024
HEADS_PER_STEP = 2
LOG2E = 1.4426950408889634
Q_SCALE = HEAD_DIM ** -0.5 * LOG2E

BF16 = jnp.bfloat16
F32 = jnp.float32


def _alibi_slopes():
    n = N_GROUPS * N_HEADS
    s = 2.0 ** (-8.0 * np.arange(1, n + 1) / n)
    return s.reshape(N_GROUPS, N_HEADS).astype(np.float32)


def _to_token_order(tm, dil):
    p = np.zeros((tm, tm), np.float32)
    u, r = np.meshgrid(np.arange(tm // dil), np.arange(dil), indexing="ij")
    p[(u * dil + r).ravel(), (r * (tm // dil) + u).ravel()] = 1.0
    return p


def _params(*semantics):
    return pltpu.CompilerParams(dimension_semantics=semantics,
                                vmem_limit_bytes=V7X_VMEM_LIMIT_BYTES)


def _rms(v, g):
    return v * lax.rsqrt(jnp.mean(v * v, axis=-1, keepdims=True) + NORM_EPS) * g


def _silu(g):
    half = 0.5 * g
    return half + half * jnp.tanh(half)


class _Tiles:
    def __init__(self, batch, seq, tm):
        self.batch, self.seq, self.tm = batch, seq, tm
        self.per_seq = seq // tm
        self.n = batch * self.per_seq

    def tile(self, s, lag):
        return jnp.clip(s - lag, 0, self.n - 1)

    def tok(self, width, lag, col=0):
        return pl.BlockSpec((self.tm, width), lambda s: (self.tile(s, lag), col))

    def res(self, width, dil, lag):
        def index(s):
            j = self.tile(s, lag)
            return (j // self.per_seq, 0, j % self.per_seq, 0)
        return pl.BlockSpec((None, dil, self.tm // dil, width), index)

    def hn_out(self, emit, lag):
        specs, shapes = [], []
        if emit >= 1:
            specs.append(self.tok(D_MODEL, lag))
            shapes.append(jax.ShapeDtypeStruct((self.batch * self.seq, D_MODEL), BF16))
        if emit >= 2:
            for dil in DILATIONS[1:]:
                specs.append(self.res(D_MODEL, dil, lag))
                shapes.append(jax.ShapeDtypeStruct(
                    (self.batch, dil, self.seq // dil, D_MODEL), BF16))
        return specs, shapes


def _const_spec(shape, block_index=None):
    index = block_index or (0,) * len(shape)
    return pl.BlockSpec(shape, lambda s: index, pipeline_mode=pl.Buffered(1))


def _perm_inputs(tm, wanted):
    if not wanted:
        return [], []
    mats = []
    for dil in DILATIONS[1:]:
        p = _to_token_order(tm, dil)
        mats += [p, p.T]
    return [jnp.asarray(m, BF16) for m in mats], [_const_spec((tm, tm))] * len(mats)


def _emit_hn(hn, perm_refs, out_refs):
    if not out_refs:
        return
    hb = hn.astype(BF16)
    out_refs[0][...] = hb
    for k, o_ref in enumerate(out_refs[1:]):
        to_res = perm_refs[2 * k + 1][...]
        res = jnp.dot(to_res, hb, preferred_element_type=F32).astype(BF16)
        dil, rows = o_ref.shape[0], o_ref.shape[1]
        for r in range(dil):
            o_ref[r] = res[r * rows:(r + 1) * rows]


def _norm_kernel(x_ref, g_ref, *refs):
    perm_refs, out_refs = refs[:4], refs[4:]
    _emit_hn(_rms(x_ref[...], g_ref[...]), perm_refs, out_refs)


def _prenorm(x2, g, tiles):
    perms, perm_specs = _perm_inputs(tiles.tm, True)
    out_specs, out_shapes = tiles.hn_out(2, 0)
    return pl.pallas_call(
        _norm_kernel,
        grid=(tiles.n,),
        in_specs=[tiles.tok(D_MODEL, 0), _const_spec((1, D_MODEL))] + perm_specs,
        out_specs=out_specs,
        out_shape=out_shapes,
        compiler_params=_params("parallel"),
        name="prenorm",
    )(x2, g.reshape(1, D_MODEL), *perms)


def _project(hn, w_ref, scale=None):
    w = w_ref[...] if scale is None else w_ref[...] * scale
    return jnp.dot(hn, w.astype(BF16), preferred_element_type=F32)


def _attn_bias_table(g, dil):
    qi = np.arange(BLOCK)[:, None]
    kj = np.arange(2 * BLOCK)[None, :]
    dist = qi + BLOCK - kj
    in_window = (dist >= 0) & (dist <= BLOCK)
    slopes = _alibi_slopes()[g].astype(np.float64) * LOG2E
    bias = -(slopes[:, None, None] * (dil * dist)[None].astype(np.float64))
    general = np.where(in_window[None], bias, -np.inf)
    first = np.where((kj >= BLOCK)[None], general, -np.inf)
    return np.stack([first, general]).astype(np.float32)


def _attend(q, kp, kc, vp, vc, bias):
    k = jnp.concatenate([kp, kc], axis=0)
    v = jnp.concatenate([vp, vc], axis=0)
    s = lax.dot_general(q, k, (((1,), (1,)), ((), ())), preferred_element_type=F32) + bias
    m = jnp.max(s, axis=-1, keepdims=True)
    p = jnp.exp2(s - m)
    l = jnp.sum(p, axis=-1, keepdims=True)
    return jnp.dot(p.astype(BF16), v, preferred_element_type=F32), m, l


def _proj_attn_kernel(hn_ref, wq_ref, wk_ref, wv_ref, bias_ref, o_ref, m_ref, l_ref, slab_ref,
                      halo_ref, *, tiles_per_seq):
    i, hp = pl.program_id(0), pl.program_id(1)
    tr = hn_ref.shape[0]
    pair = HEADS_PER_STEP * HEAD_DIM

    @pl.when((i == 0) & (hp == 0))
    def _():
        slab_ref[...] = jnp.zeros_like(slab_ref)
        halo_ref[...] = jnp.zeros_like(halo_ref)

    @pl.when(hp == 0)
    def _():
        m_ref[...] = jnp.zeros_like(m_ref)
        l_ref[...] = jnp.ones_like(l_ref)

    first = ((i - 1) % tiles_per_seq) == 0
    lane = lax.broadcasted_iota(jnp.int32, (BLOCK, HEAD_DIM), 1)

    def cols(part, hh):
        return slice(part * pair + hh * HEAD_DIM, part * pair + (hh + 1) * HEAD_DIM)

    for qb in range(tr // BLOCK):
        rows = slice(qb * BLOCK, (qb + 1) * BLOCK)
        prows = slice((qb - 1) * BLOCK, qb * BLOCK)
        m_t, l_t = m_ref[rows, :], l_ref[rows, :]
        for hh in range(HEADS_PER_STEP):
            if qb == 0:
                kp, vp = halo_ref[hp, :, cols(0, hh)], halo_ref[hp, :, cols(1, hh)]
                bias = jnp.where(first, bias_ref[0, hh], bias_ref[1, hh])
            else:
                kp, vp = slab_ref[hp, prows, cols(1, hh)], slab_ref[hp, prows, cols(2, hh)]
                bias = bias_ref[1, hh]
            o, m, l = _attend(slab_ref[hp, rows, cols(0, hh)], kp, slab_ref[hp, rows, cols(1, hh)],
                              vp, slab_ref[hp, rows, cols(2, hh)], bias)
            o_ref[rows, hh * HEAD_DIM:(hh + 1) * HEAD_DIM] = o.astype(o_ref.dtype)
            head = hp * HEADS_PER_STEP + hh
            m_t = jnp.where(lane == head, m, m_t)
            l_t = jnp.where(lane == head, l, l_t)
        m_ref[rows, :] = m_t
        l_ref[rows, :] = l_t
    halo_ref[hp] = slab_ref[hp, tr - BLOCK:tr, pair:3 * pair]
    hn = hn_ref[...]
    for part, (w_ref, scale) in enumerate(((wq_ref, Q_SCALE), (wk_ref, None), (wv_ref, None))):
        slab_ref[hp, :, part * pair:(part + 1) * pair] = _project(
            hn, w_ref, scale).astype(slab_ref.dtype)


def _proj_attention(hn, w, g, dil, seq):
    t = hn.shape[0]
    l_sub = seq // dil
    tr = min(ATTN_ROWS, l_sub)
    n_tiles = t // tr
    pair = HEADS_PER_STEP * HEAD_DIM
    n_pairs = N_HEADS // HEADS_PER_STEP
    bias = jnp.asarray(_attn_bias_table(g, dil))

    def tile_done(i, hp):
        return (jnp.maximum(i - 1, 0), 0)

    def w_spec(part):
        first = (part * N_GROUPS + g) * n_pairs
        return pl.BlockSpec((D_MODEL, pair), lambda i, hp: (0, first + hp))

    kern = functools.partial(_proj_attn_kernel, tiles_per_seq=l_sub // tr)
    return pl.pallas_call(
        kern,
        grid=(n_tiles + 1, n_pairs),
        in_specs=[pl.BlockSpec((tr, D_MODEL), lambda i, hp: (jnp.minimum(i, n_tiles - 1), 0)),
                  w_spec(0), w_spec(1), w_spec(2),
                  pl.BlockSpec((2, HEADS_PER_STEP, BLOCK, 2 * BLOCK),
                               lambda i, hp: (0, hp, 0, 0))],
        out_specs=[pl.BlockSpec((tr, pair),
                                lambda i, hp: (jnp.maximum(i - 1, 0), jnp.where(i == 0, 0, hp))),
                   pl.BlockSpec((tr, HEAD_DIM), tile_done),
                   pl.BlockSpec((tr, HEAD_DIM), tile_done)],
        out_shape=[jax.ShapeDtypeStruct((t, ATTN_WIDTH), BF16),
                   jax.ShapeDtypeStruct((t, HEAD_DIM), F32),
                   jax.ShapeDtypeStruct((t, HEAD_DIM), F32)],
        scratch_shapes=[pltpu.VMEM((n_pairs, tr, 3 * pair), BF16),
                        pltpu.VMEM((n_pairs, BLOCK, 2 * pair), BF16)],
        compiler_params=_params("arbitrary", "arbitrary"),
        name=f"proj_attn{g}",
    )(hn, w, w, w, bias)


def _finish(y_ref, x_ref, wout_ref, gpost_ref, gnext_ref, perm_refs, xo_ref, hn_refs):
    out = jnp.dot(y_ref[...], wout_ref[...], preferred_element_type=F32)
    xn = x_ref[...] + _rms(out, gpost_ref[...])
    xo_ref[...] = xn
    if hn_refs:
        _emit_hn(_rms(xn, gnext_ref[...]), perm_refs, hn_refs)


def _zero_on_first_step(y_ref):
    @pl.when(pl.program_id(0) == 0)
    def _():
        y_ref[...] = jnp.zeros_like(y_ref)


def _attn_out_kernel(o0_ref, o1_ref, o2_ref, m0_ref, m1_ref, m2_ref, l0_ref, l1_ref, l2_ref,
                     hn_ref, x_ref, wgate_ref, wout_ref, gpost_ref, gnext_ref, *refs):
    perm_refs, xo_ref, hn_refs, y_ref = refs[:4], refs[4], refs[5:-1], refs[-1]
    tm = x_ref.shape[0]
    _zero_on_first_step(y_ref)
    _finish(y_ref, x_ref, wout_ref, gpost_ref, gnext_ref, perm_refs, xo_ref, hn_refs)
    gate = jnp.dot(hn_ref[...], wgate_ref[...], preferred_element_type=F32)
    ms = [m0_ref[...], m1_ref[...], m2_ref[...]]
    m = jnp.maximum(jnp.maximum(ms[0], ms[1]), ms[2])
    es = [jnp.exp2(t - m) for t in ms]
    den = es[0] * l0_ref[...] + es[1] * l1_ref[...] + es[2] * l2_ref[...]
    ws = [e / den for e in es]
    os_ = [o0_ref[...].astype(F32)]
    for k, o_ref in enumerate((o1_ref, o2_ref)):
        res = o_ref[...].reshape(tm, ATTN_WIDTH)
        os_.append(jnp.dot(perm_refs[2 * k][...], res, preferred_element_type=F32))
    for h in range(N_HEADS):
        hs = slice(h * HEAD_DIM, (h + 1) * HEAD_DIM)
        acc = ws[0][:, h:h + 1] * os_[0][:, hs]
        for g in range(1, N_GROUPS):
            acc = acc + ws[g][:, h:h + 1] * os_[g][:, hs]
        y_ref[:, hs] = (acc * _silu(gate[:, hs])).astype(y_ref.dtype)


def _attn_epilogue(os_, stats, hn, x2, wgate, wout, gpost, gnext, tiles, emit):
    tm = tiles.tm
    perms, perm_specs = _perm_inputs(tm, True)
    hn_specs, hn_shapes = tiles.hn_out(emit, 1)
    return pl.pallas_call(
        _attn_out_kernel,
        grid=(tiles.n + 1,),
        in_specs=[tiles.tok(ATTN_WIDTH, 0), tiles.res(ATTN_WIDTH, DILATIONS[1], 0),
                  tiles.res(ATTN_WIDTH, DILATIONS[2], 0)]
                 + [tiles.tok(HEAD_DIM, 0)] * (2 * N_GROUPS)
                 + [tiles.tok(D_MODEL, 0), tiles.tok(D_MODEL, 1),
                    _const_spec((D_MODEL, ATTN_WIDTH)),
                    _const_spec((ATTN_WIDTH, D_MODEL)), _const_spec((1, D_MODEL)),
                    _const_spec((1, D_MODEL))] + perm_specs,
        out_specs=[tiles.tok(D_MODEL, 1)] + hn_specs,
        out_shape=[jax.ShapeDtypeStruct(x2.shape, F32)] + hn_shapes,
        scratch_shapes=[pltpu.VMEM((tm, ATTN_WIDTH), BF16)],
        compiler_params=_params("arbitrary"),
        name="attn_epilogue",
    )(*os_, *stats, hn, x2, wgate, wout, gpost.reshape(1, -1), gnext.reshape(1, -1), *perms)


def _conv_mix_kernel(hn_ref, wh_ref, wb_ref, wc_ref, wg_ref, cw_ref, y_ref, halo_ref, *,
                     tiles_per_seq):
    i, j = pl.program_id(0), pl.program_id(1)
    tr = hn_ref.shape[0]

    @pl.when((i == 0) & (j == 0))
    def _():
        halo_ref[...] = jnp.zeros_like(halo_ref)

    hn = hn_ref[...]
    u = _project(hn, wc_ref) * _project(hn, wh_ref)
    first = (i % tiles_per_seq) == 0
    ue = jnp.concatenate([jnp.where(first, 0.0, halo_ref[j]), u], axis=0)
    um1 = ue[HALO_ROWS - 1:HALO_ROWS - 1 + tr]
    um2 = ue[HALO_ROWS - 2:HALO_ROWS - 2 + tr]
    halo_ref[j] = u[tr - HALO_ROWS:tr]
    cw = cw_ref[...]
    conv = cw[0:1] * um2 + cw[1:2] * um1 + cw[2:3] * u
    y = _project(hn, wb_ref) * conv * _silu(_project(hn, wg_ref))
    y_ref[...] = y.astype(y_ref.dtype)


def _conv_mix(hn, w, conv_w, seq):
    t = hn.shape[0]
    tr = min(PROJ_ROWS, seq)
    n_chunks = D_MODEL // CONV_COLS

    def w_spec(part):
        return pl.BlockSpec((D_MODEL, CONV_COLS), lambda i, j: (0, part * n_chunks + j))

    kern = functools.partial(_conv_mix_kernel, tiles_per_seq=seq // tr)
    return pl.pallas_call(
        kern,
        grid=(t // tr, n_chunks),
        in_specs=[pl.BlockSpec((tr, D_MODEL), lambda i, j: (i, 0)),
                  w_spec(0), w_spec(1), w_spec(2), w_spec(3),
                  pl.BlockSpec((CONV_K, CONV_COLS), lambda i, j: (0, j))],
        out_specs=pl.BlockSpec((tr, CONV_COLS), lambda i, j: (i, j)),
        out_shape=jax.ShapeDtypeStruct((t, D_MODEL), BF16),
        scratch_shapes=[pltpu.VMEM((n_chunks, HALO_ROWS, CONV_COLS), F32)],
        compiler_params=_params("arbitrary", "arbitrary"),
        name="conv_mix",
    )(hn, w, w, w, w, conv_w)


def _out_kernel(y_ref, x_ref, wout_ref, gpost_ref, gnext_ref, *refs, n_perm):
    perm_refs, xo_ref, hn_refs = refs[:n_perm], refs[n_perm], refs[n_perm + 1:]
    _finish(y_ref, x_ref, wout_ref, gpost_ref, gnext_ref, perm_refs, xo_ref, hn_refs)


def _out_epilogue(y, x2, wout, gpost, gnext, tiles, emit):
    perms, perm_specs = _perm_inputs(tiles.tm, emit >= 2)
    hn_specs, hn_shapes = tiles.hn_out(emit, 0)
    kern = functools.partial(_out_kernel, n_perm=len(perms))
    return pl.pallas_call(
        kern,
        grid=(tiles.n,),
        in_specs=[tiles.tok(D_MODEL, 0), tiles.tok(D_MODEL, 0),
                  _const_spec((D_MODEL, D_MODEL)), _const_spec((1, D_MODEL)),
                  _const_spec((1, D_MODEL))] + perm_specs,
        out_specs=[tiles.tok(D_MODEL, 0)] + hn_specs,
        out_shape=[jax.ShapeDtypeStruct(x2.shape, F32)] + hn_shapes,
        compiler_params=_params("parallel"),
        name="out_epilogue",
    )(y, x2, wout, gpost.reshape(1, -1), gnext.reshape(1, -1), *perms)


def _attention_layer(x2, hns, w_in, w_out, gpost, gnext, tiles, emit):
    t = x2.shape[0]
    batch, seq = tiles.batch, tiles.seq
    os_, ms, ls = [], [], []
    for g, dil in enumerate(DILATIONS):
        o, m, l = _proj_attention(hns[g].reshape(t, D_MODEL), w_in, g, dil, seq)
        os_.append(o if dil == 1 else o.reshape(batch, dil, seq // dil, ATTN_WIDTH))

        def to_token_order(a, dil=dil):
            a = a.reshape(batch, dil, seq // dil, HEAD_DIM)
            return jnp.swapaxes(a, 1, 2).reshape(t, HEAD_DIM)

        ms.append(to_token_order(m))
        ls.append(to_token_order(l))
    w_gate = w_in[:, 3 * N_GROUPS * ATTN_WIDTH:].astype(BF16)
    return _attn_epilogue(os_, ms + ls, hns[0], x2, w_gate, w_out.astype(BF16), gpost, gnext,
                          tiles, emit)


def _conv_layer(x2, hn, w_in, conv_w, w_out, gpost, gnext, tiles, emit):
    y = _conv_mix(hn, w_in, conv_w, tiles.seq)
    return _out_epilogue(y, x2, w_out.astype(BF16), gpost, gnext, tiles, emit)


def kernel(x, l0_norm_pre, l0_w_in, l0_w_out, l0_norm_post, l1_norm_pre, l1_w_in, l1_conv_w, l1_w_out, l1_norm_post, l2_norm_pre, l2_w_in, l2_w_out, l2_norm_post, l3_norm_pre, l3_w_in, l3_conv_w, l3_w_out, l3_norm_post):
    batch, seq, d = x.shape
    assert d == D_MODEL and seq % (DILATIONS[-1] * BLOCK) == 0
    tiles = _Tiles(batch, seq, EPILOGUE_TOKENS)
    x2 = x.reshape(batch * seq, D_MODEL)
    hns = _prenorm(x2, l0_norm_pre, tiles)
    x2, hn = _attention_layer(x2, hns, l0_w_in, l0_w_out, l0_norm_post, l1_norm_pre, tiles, 1)
    x2, *hns = _conv_layer(x2, hn, l1_w_in, l1_conv_w, l1_w_out, l1_norm_post, l2_norm_pre, tiles, 2)
    x2, hn = _attention_layer(x2, hns, l2_w_in, l2_w_out, l2_norm_post, l3_norm_pre, tiles, 1)
    (x2,) = _conv_layer(x2, hn, l3_w_in, l3_conv_w, l3_w_out, l3_norm_post, l3_norm_pre, tiles, 0)
    return x2.reshape(batch, seq, D_MODEL)
```

```python
import functools

import jax
import jax.numpy as jnp
import numpy as np
from jax import lax
from jax.experimental import pallas as pl
from jax.experimental.pallas import tpu as pltpu

D_MODEL = 2048
HEAD_DIM = 128
N_HEADS = 16
ATTN_WIDTH = N_HEADS * HEAD_DIM
DILATIONS = (1, 4, 16)
N_GROUPS = len(DILATIONS)
BLOCK = 128
CONV_K = 3
NORM_EPS = 1e-6
HALO_ROWS = 8
V7X_VMEM_LIMIT_BYTES = 56 * 1024 * 1024
EPILOGUE_TOKENS = 256
PROJ_ROWS = 1024
CONV_COLS = 256
ATTN_ROWS = 1024
HEADS_PER_STEP = 2
MERGE_HEADS = 2
LOG2E = 1.4426950408889634
Q_SCALE = HEAD_DIM ** -0.5 * LOG2E

BF16 = jnp.bfloat16
F32 = jnp.float32


def _alibi_slopes():
    n = N_GROUPS * N_HEADS
    s = 2.0 ** (-8.0 * np.arange(1, n + 1) / n)
    return s.reshape(N_GROUPS, N_HEADS).astype(np.float32)


def _to_token_order(tm, dil):
    p = np.zeros((tm, tm), np.float32)
    u, r = np.meshgrid(np.arange(tm // dil), np.arange(dil), indexing="ij")
    p[(u * dil + r).ravel(), (r * (tm // dil) + u).ravel()] = 1.0
    return p


def _params(*semantics):
    return pltpu.CompilerParams(dimension_semantics=semantics,
                                vmem_limit_bytes=V7X_VMEM_LIMIT_BYTES)


def _rms(v, g):
    return v * lax.rsqrt(jnp.mean(v * v, axis=-1, keepdims=True) + NORM_EPS) * g


def _silu(g):
    half = 0.5 * g
    return half + half * jnp.tanh(half)


class _Tiles:
    def __init__(self, batch, seq, tm):
        self.batch, self.seq, self.tm = batch, seq, tm
        self.per_seq = seq // tm
        self.n = batch * self.per_seq

    def tile(self, s, lag):
        return jnp.clip(s - lag, 0, self.n - 1)

    def tok(self, width, lag, col=0):
        return pl.BlockSpec((self.tm, width), lambda s: (self.tile(s, lag), col))

    def res(self, width, dil, lag):
        def index(s):
            j = self.tile(s, lag)
            return (j // self.per_seq, 0, j % self.per_seq, 0)
        return pl.BlockSpec((None, dil, self.tm // dil, width), index)

    def hn_out(self, emit, lag):
        specs, shapes = [], []
        if emit >= 1:
            specs.append(self.tok(D_MODEL, lag))
            shapes.append(jax.ShapeDtypeStruct((self.batch * self.seq, D_MODEL), BF16))
        if emit >= 2:
            for dil in DILATIONS[1:]:
                specs.append(self.res(D_MODEL, dil, lag))
                shapes.append(jax.ShapeDtypeStruct(
                    (self.batch, dil, self.seq // dil, D_MODEL), BF16))
        return specs, shapes


def _const_spec(shape, block_index=None):
    index = block_index or (0,) * len(shape)
    return pl.BlockSpec(shape, lambda s: index, pipeline_mode=pl.Buffered(1))


def _perm_inputs(tm, wanted):
    if not wanted:
        return [], []
    mats = []
    for dil in DILATIONS[1:]:
        p = _to_token_order(tm, dil)
        mats += [p, p.T]
    return [jnp.asarray(m, BF16) for m in mats], [_const_spec((tm, tm))] * len(mats)


def _emit_hn(hn, perm_refs, out_refs):
    if not out_refs:
        return
    hb = hn.astype(BF16)
    out_refs[0][...] = hb
    for k, o_ref in enumerate(out_refs[1:]):
        to_res = perm_refs[2 * k + 1][...]
        res = jnp.dot(to_res, hb, preferred_element_type=F32).astype(BF16)
        dil, rows = o_ref.shape[0], o_ref.shape[1]
        for r in range(dil):
            o_ref[r] = res[r * rows:(r + 1) * rows]


def _norm_kernel(x_ref, g_ref, *refs):
    perm_refs, out_refs = refs[:4], refs[4:]
    _emit_hn(_rms(x_ref[...], g_ref[...]), perm_refs, out_refs)


def _prenorm(x2, g, tiles):
    perms, perm_specs = _perm_inputs(tiles.tm, True)
    out_specs, out_shapes = tiles.hn_out(2, 0)
    return pl.pallas_call(
        _norm_kernel,
        grid=(tiles.n,),
        in_specs=[tiles.tok(D_MODEL, 0), _const_spec((1, D_MODEL))] + perm_specs,
        out_specs=out_specs,
        out_shape=out_shapes,
        compiler_params=_params("parallel"),
        name="prenorm",
    )(x2, g.reshape(1, D_MODEL), *perms)


def _project(hn, w_ref, scale=None):
    w = w_ref[...] if scale is None else w_ref[...] * scale
    return jnp.dot(hn, w.astype(BF16), preferred_element_type=F32)


def _attn_bias_table(g, dil):
    qi = np.arange(BLOCK)[:, None]
    kj = np.arange(2 * BLOCK)[None, :]
    dist = qi + BLOCK - kj
    in_window = (dist >= 0) & (dist <= BLOCK)
    slopes = _alibi_slopes()[g].astype(np.float64) * LOG2E
    bias = -(slopes[:, None, None] * (dil * dist)[None].astype(np.float64))
    general = np.where(in_window[None], bias, -np.inf)
    first = np.where((kj >= BLOCK)[None], general, -np.inf)
    return np.stack([first, general]).astype(np.float32)


def _attend(q, kp, kc, vp, vc, bias):
    k = jnp.concatenate([kp, kc], axis=0)
    v = jnp.concatenate([vp, vc], axis=0)
    s = lax.dot_general(q, k, (((1,), (1,)), ((), ())), preferred_element_type=F32) + bias
    m = jnp.max(s, axis=-1, keepdims=True)
    p = jnp.exp2(s - m)
    l = jnp.sum(p, axis=-1, keepdims=True)
    return jnp.dot(p.astype(BF16), v, preferred_element_type=F32), m, l


def _proj_attn_kernel(hn_ref, wq_ref, wk_ref, wv_ref, bias_ref, o_ref, m_ref, l_ref, slab_ref,
                      halo_ref, *, tiles_per_seq):
    i, hp = pl.program_id(0), pl.program_id(1)
    last = pl.num_programs(0) - 1
    tr = hn_ref.shape[0]
    pair = HEADS_PER_STEP * HEAD_DIM

    @pl.when((i == 0) & (hp == 0))
    def _():
        halo_ref[...] = jnp.zeros_like(halo_ref)

    @pl.when(hp == 0)
    def _():
        m_ref[...] = jnp.zeros_like(m_ref)
        l_ref[...] = jnp.ones_like(l_ref)

    def cols(part, hh):
        return slice(part * pair + hh * HEAD_DIM, part * pair + (hh + 1) * HEAD_DIM)

    def attention():
        first = ((i - 1) % tiles_per_seq) == 0
        lane = lax.broadcasted_iota(jnp.int32, (BLOCK, HEAD_DIM), 1)
        for qb in range(tr // BLOCK):
            rows = slice(qb * BLOCK, (qb + 1) * BLOCK)
            prows = slice((qb - 1) * BLOCK, qb * BLOCK)
            m_t, l_t = m_ref[rows, :], l_ref[rows, :]
            for hh in range(HEADS_PER_STEP):
                if qb == 0:
                    kp, vp = halo_ref[hp, :, cols(0, hh)], halo_ref[hp, :, cols(1, hh)]
                    bias = jnp.where(first, bias_ref[0, hh], bias_ref[1, hh])
                else:
                    kp, vp = slab_ref[hp, prows, cols(1, hh)], slab_ref[hp, prows, cols(2, hh)]
                    bias = bias_ref[1, hh]
                o, m, l = _attend(slab_ref[hp, rows, cols(0, hh)], kp,
                                  slab_ref[hp, rows, cols(1, hh)], vp,
                                  slab_ref[hp, rows, cols(2, hh)], bias)
                o_ref[rows, hh * HEAD_DIM:(hh + 1) * HEAD_DIM] = o.astype(o_ref.dtype)
                head = hp * HEADS_PER_STEP + hh
                m_t = jnp.where(lane == head, m, m_t)
                l_t = jnp.where(lane == head, l, l_t)
            m_ref[rows, :] = m_t
            l_ref[rows, :] = l_t
        halo_ref[hp] = slab_ref[hp, tr - BLOCK:tr, pair:3 * pair]

    def projection():
        hn = hn_ref[...]
        for part, (w_ref, scale) in enumerate(((wq_ref, Q_SCALE), (wk_ref, None), (wv_ref, None))):
            slab_ref[hp, :, part * pair:(part + 1) * pair] = _project(
                hn, w_ref, scale).astype(slab_ref.dtype)

    @pl.when(i == 0)
    def _():
        projection()

    @pl.when((i > 0) & (i < last))
    def _():
        attention()
        projection()

    @pl.when(i == last)
    def _():
        attention()


def _proj_attention(hn, w, g, dil, seq):
    t = hn.shape[0]
    l_sub = seq // dil
    tr = min(ATTN_ROWS, l_sub)
    n_tiles = t // tr
    pair = HEADS_PER_STEP * HEAD_DIM
    n_pairs = N_HEADS // HEADS_PER_STEP
    bias = jnp.asarray(_attn_bias_table(g, dil))

    def tile_done(i, hp):
        return (jnp.maximum(i - 1, 0), 0)

    def w_spec(part):
        first = (part * N_GROUPS + g) * n_pairs
        return pl.BlockSpec((D_MODEL, pair), lambda i, hp: (0, first + hp))

    kern = functools.partial(_proj_attn_kernel, tiles_per_seq=l_sub // tr)
    return pl.pallas_call(
        kern,
        grid=(n_tiles + 1, n_pairs),
        in_specs=[pl.BlockSpec((tr, D_MODEL), lambda i, hp: (jnp.minimum(i, n_tiles - 1), 0)),
                  w_spec(0), w_spec(1), w_spec(2),
                  pl.BlockSpec((2, HEADS_PER_STEP, BLOCK, 2 * BLOCK),
                               lambda i, hp: (0, hp, 0, 0))],
        out_specs=[pl.BlockSpec((tr, pair),
                                lambda i, hp: (jnp.maximum(i - 1, 0), jnp.where(i == 0, 0, hp))),
                   pl.BlockSpec((tr, HEAD_DIM), tile_done),
                   pl.BlockSpec((tr, HEAD_DIM), tile_done)],
        out_shape=[jax.ShapeDtypeStruct((t, ATTN_WIDTH), BF16),
                   jax.ShapeDtypeStruct((t, HEAD_DIM), F32),
                   jax.ShapeDtypeStruct((t, HEAD_DIM), F32)],
        scratch_shapes=[pltpu.VMEM((n_pairs, tr, 3 * pair), BF16),
                        pltpu.VMEM((n_pairs, BLOCK, 2 * pair), BF16)],
        compiler_params=_params("arbitrary", "arbitrary"),
        name=f"proj_attn{g}",
    )(hn, w, w, w, bias)


def _finish(y_ref, x_ref, wout_ref, gpost_ref, gnext_ref, perm_refs, xo_ref, hn_refs):
    out = jnp.dot(y_ref[...], wout_ref[...], preferred_element_type=F32)
    xn = x_ref[...] + _rms(out, gpost_ref[...])
    xo_ref[...] = xn
    if hn_refs:
        _emit_hn(_rms(xn, gnext_ref[...]), perm_refs, hn_refs)


def _zero_on_first_step(y_ref):
    @pl.when(pl.program_id(0) == 0)
    def _():
        y_ref[...] = jnp.zeros_like(y_ref)


def _attn_out_kernel(o0_ref, o1_ref, o2_ref, m0_ref, m1_ref, m2_ref, l0_ref, l1_ref, l2_ref,
                     hn_ref, x_ref, wgate_ref, wout_ref, gpost_ref, gnext_ref, *refs):
    perm_refs, xo_ref, hn_refs, y_ref = refs[:4], refs[4], refs[5:-1], refs[-1]
    tm = x_ref.shape[0]
    _zero_on_first_step(y_ref)
    _finish(y_ref, x_ref, wout_ref, gpost_ref, gnext_ref, perm_refs, xo_ref, hn_refs)
    ms = [m0_ref[...], m1_ref[...], m2_ref[...]]
    m = jnp.maximum(jnp.maximum(ms[0], ms[1]), ms[2])
    es = [jnp.exp2(t - m) for t in ms]
    den = es[0] * l0_ref[...] + es[1] * l1_ref[...] + es[2] * l2_ref[...]
    ws = [e / den for e in es]
    hn = hn_ref[...]
    pair = MERGE_HEADS * HEAD_DIM
    for hp in range(N_HEADS // MERGE_HEADS):
        ps = slice(hp * pair, (hp + 1) * pair)
        gate = jnp.dot(hn, wgate_ref[:, ps], preferred_element_type=F32)
        os_ = [o0_ref[:, ps].astype(F32)]
        for k, o_ref in enumerate((o1_ref, o2_ref)):
            res = o_ref[:, :, ps].reshape(tm, pair)
            os_.append(jnp.dot(perm_refs[2 * k][...], res, preferred_element_type=F32))
        for hh in range(MERGE_HEADS):
            h = hp * MERGE_HEADS + hh
            sub = slice(hh * HEAD_DIM, (hh + 1) * HEAD_DIM)
            acc = ws[0][:, h:h + 1] * os_[0][:, sub]
            for g in range(1, N_GROUPS):
                acc = acc + ws[g][:, h:h + 1] * os_[g][:, sub]
            y_ref[:, h * HEAD_DIM:(h + 1) * HEAD_DIM] = (
                acc * _silu(gate[:, sub])).astype(y_ref.dtype)


def _attn_epilogue(os_, stats, hn, x2, wgate, wout, gpost, gnext, tiles, emit):
    tm = tiles.tm
    perms, perm_specs = _perm_inputs(tm, True)
    hn_specs, hn_shapes = tiles.hn_out(emit, 1)
    return pl.pallas_call(
        _attn_out_kernel,
        grid=(tiles.n + 1,),
        in_specs=[tiles.tok(ATTN_WIDTH, 0), tiles.res(ATTN_WIDTH, DILATIONS[1], 0),
                  tiles.res(ATTN_WIDTH, DILATIONS[2], 0)]
                 + [tiles.tok(HEAD_DIM, 0)] * (2 * N_GROUPS)
                 + [tiles.tok(D_MODEL, 0), tiles.tok(D_MODEL, 1),
                    _const_spec((D_MODEL, ATTN_WIDTH)),
                    _const_spec((ATTN_WIDTH, D_MODEL)), _const_spec((1, D_MODEL)),
                    _const_spec((1, D_MODEL))] + perm_specs,
        out_specs=[tiles.tok(D_MODEL, 1)] + hn_specs,
        out_shape=[jax.ShapeDtypeStruct(x2.shape, F32)] + hn_shapes,
        scratch_shapes=[pltpu.VMEM((tm, ATTN_WIDTH), BF16)],
        compiler_params=_params("arbitrary"),
        name="attn_epilogue",
    )(*os_, *stats, hn, x2, wgate, wout, gpost.reshape(1, -1), gnext.reshape(1, -1), *perms)


def _conv_mix_kernel(hn_ref, wh_ref, wb_ref, wc_ref, wg_ref, cw_ref, y_ref, halo_ref, *,
                     tiles_per_seq):
    i, j = pl.program_id(0), pl.program_id(1)
    tr = hn_ref.shape[0]

    @pl.when((i == 0) & (j == 0))
    def _():
        halo_ref[...] = jnp.zeros_like(halo_ref)

    hn = hn_ref[...]
    u = _project(hn, wc_ref) * _project(hn, wh_ref)
    first = (i % tiles_per_seq) == 0
    ue = jnp.concatenate([jnp.where(first, 0.0, halo_ref[j]), u], axis=0)
    um1 = ue[HALO_ROWS - 1:HALO_ROWS - 1 + tr]
    um2 = ue[HALO_ROWS - 2:HALO_ROWS - 2 + tr]
    halo_ref[j] = u[tr - HALO_ROWS:tr]
    cw = cw_ref[...]
    conv = cw[0:1] * um2 + cw[1:2] * um1 + cw[2:3] * u
    y = _project(hn, wb_ref) * conv * _silu(_project(hn, wg_ref))
    y_ref[...] = y.astype(y_ref.dtype)


def _conv_mix(hn, w, conv_w, seq):
    t = hn.shape[0]
    tr = min(PROJ_ROWS, seq)
    n_chunks = D_MODEL // CONV_COLS

    def w_spec(part):
        return pl.BlockSpec((D_MODEL, CONV_COLS), lambda i, j: (0, part * n_chunks + j))

    kern = functools.partial(_conv_mix_kernel, tiles_per_seq=seq // tr)
    return pl.pallas_call(
        kern,
        grid=(t // tr, n_chunks),
        in_specs=[pl.BlockSpec((tr, D_MODEL), lambda i, j: (i, 0)),
                  w_spec(0), w_spec(1), w_spec(2), w_spec(3),
                  pl.BlockSpec((CONV_K, CONV_COLS), lambda i, j: (0, j))],
        out_specs=pl.BlockSpec((tr, CONV_COLS), lambda i, j: (i, j)),
        out_shape=jax.ShapeDtypeStruct((t, D_MODEL), BF16),
        scratch_shapes=[pltpu.VMEM((n_chunks, HALO_ROWS, CONV_COLS), F32)],
        compiler_params=_params("arbitrary", "arbitrary"),
        name="conv_mix",
    )(hn, w, w, w, w, conv_w)


def _out_kernel(y_ref, x_ref, wout_ref, gpost_ref, gnext_ref, *refs, n_perm):
    perm_refs, xo_ref, hn_refs = refs[:n_perm], refs[n_perm], refs[n_perm + 1:]
    _finish(y_ref, x_ref, wout_ref, gpost_ref, gnext_ref, perm_refs, xo_ref, hn_refs)


def _out_epilogue(y, x2, wout, gpost, gnext, tiles, emit):
    perms, perm_specs = _perm_inputs(tiles.tm, emit >= 2)
    hn_specs, hn_shapes = tiles.hn_out(emit, 0)
    kern = functools.partial(_out_kernel, n_perm=len(perms))
    return pl.pallas_call(
        kern,
        grid=(tiles.n,),
        in_specs=[tiles.tok(D_MODEL, 0), tiles.tok(D_MODEL, 0),
                  _const_spec((D_MODEL, D_MODEL)), _const_spec((1, D_MODEL)),
                  _const_spec((1, D_MODEL))] + perm_specs,
        out_specs=[tiles.tok(D_MODEL, 0)] + hn_specs,
        out_shape=[jax.ShapeDtypeStruct(x2.shape, F32)] + hn_shapes,
        compiler_params=_params("parallel"),
        name="out_epilogue",
    )(y, x2, wout, gpost.reshape(1, -1), gnext.reshape(1, -1), *perms)


def _attention_layer(x2, hns, w_in, w_out, gpost, gnext, tiles, emit):
    t = x2.shape[0]
    batch, seq = tiles.batch, tiles.seq
    os_, ms, ls = [], [], []
    for g, dil in enumerate(DILATIONS):
        o, m, l = _proj_attention(hns[g].reshape(t, D_MODEL), w_in, g, dil, seq)
        os_.append(o if dil == 1 else o.reshape(batch, dil, seq // dil, ATTN_WIDTH))

        def to_token_order(a, dil=dil):
            a = a.reshape(batch, dil, seq // dil, HEAD_DIM)
            return jnp.swapaxes(a, 1, 2).reshape(t, HEAD_DIM)

        ms.append(to_token_order(m))
        ls.append(to_token_order(l))
    w_gate = w_in[:, 3 * N_GROUPS * ATTN_WIDTH:].astype(BF16)
    return _attn_epilogue(os_, ms + ls, hns[0], x2, w_gate, w_out.astype(BF16), gpost, gnext,
                          tiles, emit)


def _conv_layer(x2, hn, w_in, conv_w, w_out, gpost, gnext, tiles, emit):
    y = _conv_mix(hn, w_in, conv_w, tiles.seq)
    return _out_epilogue(y, x2, w_out.astype(BF16), gpost, gnext, tiles, emit)


def kernel(x, l0_norm_pre, l0_w_in, l0_w_out, l0_norm_post, l1_norm_pre, l1_w_in, l1_conv_w, l1_w_out, l1_norm_post, l2_norm_pre, l2_w_in, l2_w_out, l2_norm_post, l3_norm_pre, l3_w_in, l3_conv_w, l3_w_out, l3_norm_post):
    batch, seq, d = x.shape
    assert d == D_MODEL and seq % (DILATIONS[-1] * BLOCK) == 0
    tiles = _Tiles(batch, seq, EPILOGUE_TOKENS)
    x2 = x.reshape(batch * seq, D_MODEL)
    hns = _prenorm(x2, l0_norm_pre, tiles)
    x2, hn = _attention_layer(x2, hns, l0_w_in, l0_w_out, l0_norm_post, l1_norm_pre, tiles, 1)
    x2, *hns = _conv_layer(x2, hn, l1_w_in, l1_conv_w, l1_w_out, l1_norm_post, l2_norm_pre, tiles, 2)
    x2, hn = _attention_layer(x2, hns, l2_w_in, l2_w_out, l2_norm_post, l3_norm_pre, tiles, 1)
    (x2,) = _conv_layer(x2, hn, l3_w_in, l3_conv_w, l3_w_out, l3_norm_post, l3_norm_pre, tiles, 0)
    return x2.reshape(batch, seq, D_MODEL)
```

```python
import functools

import jax
import jax.numpy as jnp
import numpy as np
from jax import lax
from jax.experimental import pallas as pl
from jax.experimental.pallas import tpu as pltpu

D_MODEL = 2048
HEAD_DIM = 128
N_HEADS = 16
ATTN_WIDTH = N_HEADS * HEAD_DIM
DILATIONS = (1, 4, 16)
N_GROUPS = len(DILATIONS)
BLOCK = 128
CONV_K = 3
NORM_EPS = 1e-6
HALO_ROWS = 8
V7X_VMEM_LIMIT_BYTES = 56 * 1024 * 1024
EPILOGUE_TOKENS = 256
PROJ_ROWS = 1024
CONV_COLS = 256
ATTN_ROWS = 1024
HEADS_PER_STEP = 4
MERGE_HEADS = 2
LOG2E = 1.4426950408889634
Q_SCALE = HEAD_DIM ** -0.5 * LOG2E

BF16 = jnp.bfloat16
F32 = jnp.float32


def _alibi_slopes():
    n = N_GROUPS * N_HEADS
    s = 2.0 ** (-8.0 * np.arange(1, n + 1) / n)
    return s.reshape(N_GROUPS, N_HEADS).astype(np.float32)


def _to_token_order(tm, dil):
    p = np.zeros((tm, tm), np.float32)
    u, r = np.meshgrid(np.arange(tm // dil), np.arange(dil), indexing="ij")
    p[(u * dil + r).ravel(), (r * (tm // dil) + u).ravel()] = 1.0
    return p


def _params(*semantics):
    return pltpu.CompilerParams(dimension_semantics=semantics,
                                vmem_limit_bytes=V7X_VMEM_LIMIT_BYTES)


def _rms(v, g):
    return v * lax.rsqrt(jnp.mean(v * v, axis=-1, keepdims=True) + NORM_EPS) * g


def _silu(g):
    half = 0.5 * g
    return half + half * jnp.tanh(half)


class _Tiles:
    def __init__(self, batch, seq, tm):
        self.batch, self.seq, self.tm = batch, seq, tm
        self.per_seq = seq // tm
        self.n = batch * self.per_seq

    def tile(self, s, lag):
        return jnp.clip(s - lag, 0, self.n - 1)

    def tok(self, width, lag, col=0):
        return pl.BlockSpec((self.tm, width), lambda s: (self.tile(s, lag), col))

    def res(self, width, dil, lag):
        def index(s):
            j = self.tile(s, lag)
            return (j // self.per_seq, 0, j % self.per_seq, 0)
        return pl.BlockSpec((None, dil, self.tm // dil, width), index)

    def hn_out(self, emit, lag):
        specs, shapes = [], []
        if emit >= 1:
            specs.append(self.tok(D_MODEL, lag))
            shapes.append(jax.ShapeDtypeStruct((self.batch * self.seq, D_MODEL), BF16))
        if emit >= 2:
            for dil in DILATIONS[1:]:
                specs.append(self.res(D_MODEL, dil, lag))
                shapes.append(jax.ShapeDtypeStruct(
                    (self.batch, dil, self.seq // dil, D_MODEL), BF16))
        return specs, shapes


def _const_spec(shape, block_index=None):
    index = block_index or (0,) * len(shape)
    return pl.BlockSpec(shape, lambda s: index, pipeline_mode=pl.Buffered(1))


def _perm_inputs(tm, wanted):
    if not wanted:
        return [], []
    mats = []
    for dil in DILATIONS[1:]:
        p = _to_token_order(tm, dil)
        mats += [p, p.T]
    return [jnp.asarray(m, BF16) for m in mats], [_const_spec((tm, tm))] * len(mats)


def _emit_hn(hn, perm_refs, out_refs):
    if not out_refs:
        return
    hb = hn.astype(BF16)
    out_refs[0][...] = hb
    for k, o_ref in enumerate(out_refs[1:]):
        to_res = perm_refs[2 * k + 1][...]
        res = jnp.dot(to_res, hb, preferred_element_type=F32).astype(BF16)
        dil, rows = o_ref.shape[0], o_ref.shape[1]
        for r in range(dil):
            o_ref[r] = res[r * rows:(r + 1) * rows]


def _norm_kernel(x_ref, g_ref, *refs):
    perm_refs, out_refs = refs[:4], refs[4:]
    _emit_hn(_rms(x_ref[...], g_ref[...]), perm_refs, out_refs)


def _prenorm(x2, g, tiles):
    perms, perm_specs = _perm_inputs(tiles.tm, True)
    out_specs, out_shapes = tiles.hn_out(2, 0)
    return pl.pallas_call(
        _norm_kernel,
        grid=(tiles.n,),
        in_specs=[tiles.tok(D_MODEL, 0), _const_spec((1, D_MODEL))] + perm_specs,
        out_specs=out_specs,
        out_shape=out_shapes,
        compiler_params=_params("parallel"),
        name="prenorm",
    )(x2, g.reshape(1, D_MODEL), *perms)


def _project(hn, w_ref, scale=None):
    w = w_ref[...] if scale is None else w_ref[...] * scale
    return jnp.dot(hn, w.astype(BF16), preferred_element_type=F32)


def _attn_bias_table(g, dil):
    qi = np.arange(BLOCK)[:, None]
    kj = np.arange(2 * BLOCK)[None, :]
    dist = qi + BLOCK - kj
    in_window = (dist >= 0) & (dist <= BLOCK)
    slopes = _alibi_slopes()[g].astype(np.float64) * LOG2E
    bias = -(slopes[:, None, None] * (dil * dist)[None].astype(np.float64))
    general = np.where(in_window[None], bias, -np.inf)
    first = np.where((kj >= BLOCK)[None], general, -np.inf)
    return np.stack([first, general]).astype(np.float32)


def _attend(q, kp, kc, vp, vc, bias):
    k = jnp.concatenate([kp, kc], axis=0)
    v = jnp.concatenate([vp, vc], axis=0)
    s = lax.dot_general(q, k, (((1,), (1,)), ((), ())), preferred_element_type=F32) + bias
    m = jnp.max(s, axis=-1, keepdims=True)
    p = jnp.exp2(s - m)
    l = jnp.sum(p, axis=-1, keepdims=True)
    return jnp.dot(p.astype(BF16), v, preferred_element_type=F32), m, l


def _proj_attn_kernel(hn_ref, wq_ref, wk_ref, wv_ref, bias_ref, o_ref, m_ref, l_ref, slab_ref,
                      halo_ref, *, tiles_per_seq):
    i, hp = pl.program_id(0), pl.program_id(1)
    last = pl.num_programs(0) - 1
    tr = hn_ref.shape[0]
    pair = HEADS_PER_STEP * HEAD_DIM

    @pl.when((i == 0) & (hp == 0))
    def _():
        halo_ref[...] = jnp.zeros_like(halo_ref)

    @pl.when(hp == 0)
    def _():
        m_ref[...] = jnp.zeros_like(m_ref)
        l_ref[...] = jnp.ones_like(l_ref)

    def cols(part, hh):
        return slice(part * pair + hh * HEAD_DIM, part * pair + (hh + 1) * HEAD_DIM)

    def attention():
        first = ((i - 1) % tiles_per_seq) == 0
        lane = lax.broadcasted_iota(jnp.int32, (BLOCK, HEAD_DIM), 1)
        for qb in range(tr // BLOCK):
            rows = slice(qb * BLOCK, (qb + 1) * BLOCK)
            prows = slice((qb - 1) * BLOCK, qb * BLOCK)
            m_t, l_t = m_ref[rows, :], l_ref[rows, :]
            for hh in range(HEADS_PER_STEP):
                if qb == 0:
                    kp, vp = halo_ref[hp, :, cols(0, hh)], halo_ref[hp, :, cols(1, hh)]
                    bias = jnp.where(first, bias_ref[0, hh], bias_ref[1, hh])
                else:
                    kp, vp = slab_ref[hp, prows, cols(1, hh)], slab_ref[hp, prows, cols(2, hh)]
                    bias = bias_ref[1, hh]
                o, m, l = _attend(slab_ref[hp, rows, cols(0, hh)], kp,
                                  slab_ref[hp, rows, cols(1, hh)], vp,
                                  slab_ref[hp, rows, cols(2, hh)], bias)
                o_ref[rows, hh * HEAD_DIM:(hh + 1) * HEAD_DIM] = o.astype(o_ref.dtype)
                head = hp * HEADS_PER_STEP + hh
                m_t = jnp.where(lane == head, m, m_t)
                l_t = jnp.where(lane == head, l, l_t)
            m_ref[rows, :] = m_t
            l_ref[rows, :] = l_t
        halo_ref[hp] = slab_ref[hp, tr - BLOCK:tr, pair:3 * pair]

    def projection():
        hn = hn_ref[...]
        for part, (w_ref, scale) in enumerate(((wq_ref, Q_SCALE), (wk_ref, None), (wv_ref, None))):
            slab_ref[hp, :, part * pair:(part + 1) * pair] = _project(
                hn, w_ref, scale).astype(slab_ref.dtype)

    @pl.when(i == 0)
    def _():
        projection()

    @pl.when((i > 0) & (i < last))
    def _():
        attention()
        projection()

    @pl.when(i == last)
    def _():
        attention()


def _proj_attention(hn, w, g, dil, seq):
    t = hn.shape[0]
    l_sub = seq // dil
    tr = min(ATTN_ROWS, l_sub)
    n_tiles = t // tr
    pair = HEADS_PER_STEP * HEAD_DIM
    n_pairs = N_HEADS // HEADS_PER_STEP
    bias = jnp.asarray(_attn_bias_table(g, dil))

    def tile_done(i, hp):
        return (jnp.maximum(i - 1, 0), 0)

    def w_spec(part):
        first = (part * N_GROUPS + g) * n_pairs
        return pl.BlockSpec((D_MODEL, pair), lambda i, hp: (0, first + hp))

    kern = functools.partial(_proj_attn_kernel, tiles_per_seq=l_sub // tr)
    return pl.pallas_call(
        kern,
        grid=(n_tiles + 1, n_pairs),
        in_specs=[pl.BlockSpec((tr, D_MODEL), lambda i, hp: (jnp.minimum(i, n_tiles - 1), 0)),
                  w_spec(0), w_spec(1), w_spec(2),
                  pl.BlockSpec((2, HEADS_PER_STEP, BLOCK, 2 * BLOCK),
                               lambda i, hp: (0, hp, 0, 0))],
        out_specs=[pl.BlockSpec((tr, pair),
                                lambda i, hp: (jnp.maximum(i - 1, 0), jnp.where(i == 0, 0, hp))),
                   pl.BlockSpec((tr, HEAD_DIM), tile_done),
                   pl.BlockSpec((tr, HEAD_DIM), tile_done)],
        out_shape=[jax.ShapeDtypeStruct((t, ATTN_WIDTH), BF16),
                   jax.ShapeDtypeStruct((t, HEAD_DIM), F32),
                   jax.ShapeDtypeStruct((t, HEAD_DIM), F32)],
        scratch_shapes=[pltpu.VMEM((n_pairs, tr, 3 * pair), BF16),
                        pltpu.VMEM((n_pairs, BLOCK, 2 * pair), BF16)],
        compiler_params=_params("arbitrary", "arbitrary"),
        name=f"proj_attn{g}",
    )(hn, w, w, w, bias)


def _finish(y_ref, x_ref, wout_ref, gpost_ref, gnext_ref, perm_refs, xo_ref, hn_refs):
    out = jnp.dot(y_ref[...], wout_ref[...], preferred_element_type=F32)
    xn = x_ref[...] + _rms(out, gpost_ref[...])
    xo_ref[...] = xn
    if hn_refs:
        _emit_hn(_rms(xn, gnext_ref[...]), perm_refs, hn_refs)


def _zero_on_first_step(y_ref):
    @pl.when(pl.program_id(0) == 0)
    def _():
        y_ref[...] = jnp.zeros_like(y_ref)


def _attn_out_kernel(o0_ref, o1_ref, o2_ref, m0_ref, m1_ref, m2_ref, l0_ref, l1_ref, l2_ref,
                     hn_ref, x_ref, wgate_ref, wout_ref, gpost_ref, gnext_ref, *refs):
    perm_refs, xo_ref, hn_refs, y_ref = refs[:4], refs[4], refs[5:-1], refs[-1]
    tm = x_ref.shape[0]
    _zero_on_first_step(y_ref)
    _finish(y_ref, x_ref, wout_ref, gpost_ref, gnext_ref, perm_refs, xo_ref, hn_refs)
    ms = [m0_ref[...], m1_ref[...], m2_ref[...]]
    m = jnp.maximum(jnp.maximum(ms[0], ms[1]), ms[2])
    es = [jnp.exp2(t - m) for t in ms]
    den = es[0] * l0_ref[...] + es[1] * l1_ref[...] + es[2] * l2_ref[...]
    ws = [e / den for e in es]
    hn = hn_ref[...]
    pair = MERGE_HEADS * HEAD_DIM
    for hp in range(N_HEADS // MERGE_HEADS):
        ps = slice(hp * pair, (hp + 1) * pair)
        gate = jnp.dot(hn, wgate_ref[:, ps], preferred_element_type=F32)
        os_ = [o0_ref[:, ps].astype(F32)]
        for k, o_ref in enumerate((o1_ref, o2_ref)):
            res = o_ref[:, :, ps].reshape(tm, pair)
            os_.append(jnp.dot(perm_refs[2 * k][...], res, preferred_element_type=F32))
        for hh in range(MERGE_HEADS):
            h = hp * MERGE_HEADS + hh
            sub = slice(hh * HEAD_DIM, (hh + 1) * HEAD_DIM)
            acc = ws[0][:, h:h + 1] * os_[0][:, sub]
            for g in range(1, N_GROUPS):
                acc = acc + ws[g][:, h:h + 1] * os_[g][:, sub]
            y_ref[:, h * HEAD_DIM:(h + 1) * HEAD_DIM] = (
                acc * _silu(gate[:, sub])).astype(y_ref.dtype)


def _attn_epilogue(os_, stats, hn, x2, wgate, wout, gpost, gnext, tiles, emit):
    tm = tiles.tm
    perms, perm_specs = _perm_inputs(tm, True)
    hn_specs, hn_shapes = tiles.hn_out(emit, 1)
    return pl.pallas_call(
        _attn_out_kernel,
        grid=(tiles.n + 1,),
        in_specs=[tiles.tok(ATTN_WIDTH, 0), tiles.res(ATTN_WIDTH, DILATIONS[1], 0),
                  tiles.res(ATTN_WIDTH, DILATIONS[2], 0)]
                 + [tiles.tok(HEAD_DIM, 0)] * (2 * N_GROUPS)
                 + [tiles.tok(D_MODEL, 0), tiles.tok(D_MODEL, 1),
                    _const_spec((D_MODEL, ATTN_WIDTH)),
                    _const_spec((ATTN_WIDTH, D_MODEL)), _const_spec((1, D_MODEL)),
                    _const_spec((1, D_MODEL))] + perm_specs,
        out_specs=[tiles.tok(D_MODEL, 1)] + hn_specs,
        out_shape=[jax.ShapeDtypeStruct(x2.shape, F32)] + hn_shapes,
        scratch_shapes=[pltpu.VMEM((tm, ATTN_WIDTH), BF16)],
        compiler_params=_params("arbitrary"),
        name="attn_epilogue",
    )(*os_, *stats, hn, x2, wgate, wout, gpost.reshape(1, -1), gnext.reshape(1, -1), *perms)


def _conv_mix_kernel(hn_ref, wh_ref, wb_ref, wc_ref, wg_ref, cw_ref, y_ref, halo_ref, *,
                     tiles_per_seq):
    i, j = pl.program_id(0), pl.program_id(1)
    tr = hn_ref.shape[0]

    @pl.when((i == 0) & (j == 0))
    def _():
        halo_ref[...] = jnp.zeros_like(halo_ref)

    hn = hn_ref[...]
    u = _project(hn, wc_ref) * _project(hn, wh_ref)
    first = (i % tiles_per_seq) == 0
    ue = jnp.concatenate([jnp.where(first, 0.0, halo_ref[j]), u], axis=0)
    um1 = ue[HALO_ROWS - 1:HALO_ROWS - 1 + tr]
    um2 = ue[HALO_ROWS - 2:HALO_ROWS - 2 + tr]
    halo_ref[j] = u[tr - HALO_ROWS:tr]
    cw = cw_ref[...]
    conv = cw[0:1] * um2 + cw[1:2] * um1 + cw[2:3] * u
    y = _project(hn, wb_ref) * conv * _silu(_project(hn, wg_ref))
    y_ref[...] = y.astype(y_ref.dtype)


def _conv_mix(hn, w, conv_w, seq):
    t = hn.shape[0]
    tr = min(PROJ_ROWS, seq)
    n_chunks = D_MODEL // CONV_COLS

    def w_spec(part):
        return pl.BlockSpec((D_MODEL, CONV_COLS), lambda i, j: (0, part * n_chunks + j))

    kern = functools.partial(_conv_mix_kernel, tiles_per_seq=seq // tr)
    return pl.pallas_call(
        kern,
        grid=(t // tr, n_chunks),
        in_specs=[pl.BlockSpec((tr, D_MODEL), lambda i, j: (i, 0)),
                  w_spec(0), w_spec(1), w_spec(2), w_spec(3),
                  pl.BlockSpec((CONV_K, CONV_COLS), lambda i, j: (0, j))],
        out_specs=pl.BlockSpec((tr, CONV_COLS), lambda i, j: (i, j)),
        out_shape=jax.ShapeDtypeStruct((t, D_MODEL), BF16),
        scratch_shapes=[pltpu.VMEM((n_chunks, HALO_ROWS, CONV_COLS), F32)],
        compiler_params=_params("arbitrary", "arbitrary"),
        name="conv_mix",
    )(hn, w, w, w, w, conv_w)


def _out_kernel(y_ref, x_ref, wout_ref, gpost_ref, gnext_ref, *refs, n_perm):
    perm_refs, xo_ref, hn_refs = refs[:n_perm], refs[n_perm], refs[n_perm + 1:]
    _finish(y_ref, x_ref, wout_ref, gpost_ref, gnext_ref, perm_refs, xo_ref, hn_refs)


def _out_epilogue(y, x2, wout, gpost, gnext, tiles, emit):
    perms, perm_specs = _perm_inputs(tiles.tm, emit >= 2)
    hn_specs, hn_shapes = tiles.hn_out(emit, 0)
    kern = functools.partial(_out_kernel, n_perm=len(perms))
    return pl.pallas_call(
        kern,
        grid=(tiles.n,),
        in_specs=[tiles.tok(D_MODEL, 0), tiles.tok(D_MODEL, 0),
                  _const_spec((D_MODEL, D_MODEL)), _const_spec((1, D_MODEL)),
                  _const_spec((1, D_MODEL))] + perm_specs,
        out_specs=[tiles.tok(D_MODEL, 0)] + hn_specs,
        out_shape=[jax.ShapeDtypeStruct(x2.shape, F32)] + hn_shapes,
        compiler_params=_params("parallel"),
        name="out_epilogue",
    )(y, x2, wout, gpost.reshape(1, -1), gnext.reshape(1, -1), *perms)


def _attention_layer(x2, hns, w_in, w_out, gpost, gnext, tiles, emit):
    t = x2.shape[0]
    batch, seq = tiles.batch, tiles.seq
    os_, ms, ls = [], [], []
    for g, dil in enumerate(DILATIONS):
        o, m, l = _proj_attention(hns[g].reshape(t, D_MODEL), w_in, g, dil, seq)
        os_.append(o if dil == 1 else o.reshape(batch, dil, seq // dil, ATTN_WIDTH))

        def to_token_order(a, dil=dil):
            a = a.reshape(batch, dil, seq // dil, HEAD_DIM)
            return jnp.swapaxes(a, 1, 2).reshape(t, HEAD_DIM)

        ms.append(to_token_order(m))
        ls.append(to_token_order(l))
    w_gate = w_in[:, 3 * N_GROUPS * ATTN_WIDTH:].astype(BF16)
    return _attn_epilogue(os_, ms + ls, hns[0], x2, w_gate, w_out.astype(BF16), gpost, gnext,
                          tiles, emit)


def _conv_layer(x2, hn, w_in, conv_w, w_out, gpost, gnext, tiles, emit):
    y = _conv_mix(hn, w_in, conv_w, tiles.seq)
    return _out_epilogue(y, x2, w_out.astype(BF16), gpost, gnext, tiles, emit)


def kernel(x, l0_norm_pre, l0_w_in, l0_w_out, l0_norm_post, l1_norm_pre, l1_w_in, l1_conv_w, l1_w_out, l1_norm_post, l2_norm_pre, l2_w_in, l2_w_out, l2_norm_post, l3_norm_pre, l3_w_in, l3_conv_w, l3_w_out, l3_norm_post):
    batch, seq, d = x.shape
    assert d == D_MODEL and seq % (DILATIONS[-1] * BLOCK) == 0
    tiles = _Tiles(batch, seq, EPILOGUE_TOKENS)
    x2 = x.reshape(batch * seq, D_MODEL)
    hns = _prenorm(x2, l0_norm_pre, tiles)
    x2, hn = _attention_layer(x2, hns, l0_w_in, l0_w_out, l0_norm_post, l1_norm_pre, tiles, 1)
    x2, *hns = _conv_layer(x2, hn, l1_w_in, l1_conv_w, l1_w_out, l1_norm_post, l2_norm_pre, tiles, 2)
    x2, hn = _attention_layer(x2, hns, l2_w_in, l2_w_out, l2_norm_post, l3_norm_pre, tiles, 1)
    (x2,) = _conv_layer(x2, hn, l3_w_in, l3_conv_w, l3_w_out, l3_norm_post, l3_norm_pre, tiles, 0)
    return x2.reshape(batch, seq, D_MODEL)
```

```python
import functools

import jax
import jax.numpy as jnp
import numpy as np
from jax import lax
from jax.experimental import pallas as pl
from jax.experimental.pallas import tpu as pltpu

D_MODEL = 2048
HEAD_DIM = 128
N_HEADS = 16
ATTN_WIDTH = N_HEADS * HEAD_DIM
DILATIONS = (1, 4, 16)
N_GROUPS = len(DILATIONS)
BLOCK = 128
CONV_K = 3
NORM_EPS = 1e-6
HALO_ROWS = 8
V7X_VMEM_LIMIT_BYTES = 56 * 1024 * 1024
PERM_TOKENS = 256
EPILOGUE_TOKENS = PERM_TOKENS
OUT_TOKENS = 2 * PERM_TOKENS
PROJ_ROWS = 1024
CONV_COLS = 512
ATTN_ROWS = 1024
HEADS_PER_STEP = 4
MERGE_HEADS = 2
LOG2E = 1.4426950408889634
Q_SCALE = HEAD_DIM ** -0.5 * LOG2E

BF16 = jnp.bfloat16
F32 = jnp.float32


def _alibi_slopes():
    n = N_GROUPS * N_HEADS
    s = 2.0 ** (-8.0 * np.arange(1, n + 1) / n)
    return s.reshape(N_GROUPS, N_HEADS).astype(np.float32)


def _to_token_order(tm, dil):
    p = np.zeros((tm, tm), np.float32)
    u, r = np.meshgrid(np.arange(tm // dil), np.arange(dil), indexing="ij")
    p[(u * dil + r).ravel(), (r * (tm // dil) + u).ravel()] = 1.0
    return p


def _params(*semantics):
    return pltpu.CompilerParams(dimension_semantics=semantics,
                                vmem_limit_bytes=V7X_VMEM_LIMIT_BYTES)


def _rms(v, g):
    return v * lax.rsqrt(jnp.mean(v * v, axis=-1, keepdims=True) + NORM_EPS) * g


def _silu(g):
    half = 0.5 * g
    return half + half * jnp.tanh(half)


class _Tiles:
    def __init__(self, batch, seq, tm):
        self.batch, self.seq, self.tm = batch, seq, tm
        self.per_seq = seq // tm
        self.n = batch * self.per_seq

    def tile(self, s, lag):
        return jnp.clip(s - lag, 0, self.n - 1)

    def tok(self, width, lag, col=0):
        return pl.BlockSpec((self.tm, width), lambda s: (self.tile(s, lag), col))

    def res(self, width, dil, lag):
        def index(s):
            j = self.tile(s, lag)
            return (j // self.per_seq, 0, j % self.per_seq, 0)
        return pl.BlockSpec((None, dil, self.tm // dil, width), index)

    def hn_out(self, emit, lag):
        specs, shapes = [], []
        if emit >= 1:
            specs.append(self.tok(D_MODEL, lag))
            shapes.append(jax.ShapeDtypeStruct((self.batch * self.seq, D_MODEL), BF16))
        if emit >= 2:
            for dil in DILATIONS[1:]:
                specs.append(self.res(D_MODEL, dil, lag))
                shapes.append(jax.ShapeDtypeStruct(
                    (self.batch, dil, self.seq // dil, D_MODEL), BF16))
        return specs, shapes


def _const_spec(shape, block_index=None):
    index = block_index or (0,) * len(shape)
    return pl.BlockSpec(shape, lambda s: index, pipeline_mode=pl.Buffered(1))


def _perm_inputs(wanted):
    if not wanted:
        return [], []
    mats = []
    for dil in DILATIONS[1:]:
        p = _to_token_order(PERM_TOKENS, dil)
        mats += [p, p.T]
    specs = [_const_spec((PERM_TOKENS, PERM_TOKENS))] * len(mats)
    return [jnp.asarray(m, BF16) for m in mats], specs


def _emit_hn(hn, perm_refs, out_refs):
    if not out_refs:
        return
    hb = hn.astype(BF16)
    out_refs[0][...] = hb
    for k, o_ref in enumerate(out_refs[1:]):
        to_res = perm_refs[2 * k + 1][...]
        dil = o_ref.shape[0]
        rows = PERM_TOKENS // dil
        for t0 in range(0, hb.shape[0], PERM_TOKENS):
            res = jnp.dot(to_res, hb[t0:t0 + PERM_TOKENS],
                          preferred_element_type=F32).astype(BF16)
            for r in range(dil):
                o_ref[r, t0 // dil:t0 // dil + rows] = res[r * rows:(r + 1) * rows]


def _norm_kernel(x_ref, g_ref, *refs):
    perm_refs, out_refs = refs[:4], refs[4:]
    _emit_hn(_rms(x_ref[...], g_ref[...]), perm_refs, out_refs)


def _prenorm(x2, g, tiles):
    perms, perm_specs = _perm_inputs(True)
    out_specs, out_shapes = tiles.hn_out(2, 0)
    return pl.pallas_call(
        _norm_kernel,
        grid=(tiles.n,),
        in_specs=[tiles.tok(D_MODEL, 0), _const_spec((1, D_MODEL))] + perm_specs,
        out_specs=out_specs,
        out_shape=out_shapes,
        compiler_params=_params("parallel"),
        name="prenorm",
    )(x2, g.reshape(1, D_MODEL), *perms)


def _project(hn, w_ref, scale=None):
    w = w_ref[...] if scale is None else w_ref[...] * scale
    return jnp.dot(hn, w.astype(BF16), preferred_element_type=F32)


def _attn_bias_table(g, dil):
    qi = np.arange(BLOCK)[:, None]
    kj = np.arange(2 * BLOCK)[None, :]
    dist = qi + BLOCK - kj
    in_window = (dist >= 0) & (dist <= BLOCK)
    slopes = _alibi_slopes()[g].astype(np.float64) * LOG2E
    bias = -(slopes[:, None, None] * (dil * dist)[None].astype(np.float64))
    general = np.where(in_window[None], bias, -np.inf)
    first = np.where((kj >= BLOCK)[None], general, -np.inf)
    return np.stack([first, general]).astype(np.float32)


def _attend(q, kp, kc, vp, vc, bias):
    k = jnp.concatenate([kp, kc], axis=0)
    v = jnp.concatenate([vp, vc], axis=0)
    s = lax.dot_general(q, k, (((1,), (1,)), ((), ())), preferred_element_type=F32) + bias
    m = jnp.max(s, axis=-1, keepdims=True)
    p = jnp.exp2(s - m)
    l = jnp.sum(p, axis=-1, keepdims=True)
    return jnp.dot(p.astype(BF16), v, preferred_element_type=F32), m, l


def _proj_attn_kernel(hn_ref, wq_ref, wk_ref, wv_ref, bias_ref, o_ref, m_ref, l_ref, slab_ref,
                      halo_ref, *, tiles_per_seq):
    i, hp = pl.program_id(0), pl.program_id(1)
    last = pl.num_programs(0) - 1
    tr = hn_ref.shape[0]
    pair = HEADS_PER_STEP * HEAD_DIM

    @pl.when((i == 0) & (hp == 0))
    def _():
        halo_ref[...] = jnp.zeros_like(halo_ref)

    @pl.when(hp == 0)
    def _():
        m_ref[...] = jnp.zeros_like(m_ref)
        l_ref[...] = jnp.ones_like(l_ref)

    def cols(part, hh):
        return slice(part * pair + hh * HEAD_DIM, part * pair + (hh + 1) * HEAD_DIM)

    def attention():
        first = ((i - 1) % tiles_per_seq) == 0
        lane = lax.broadcasted_iota(jnp.int32, (BLOCK, HEAD_DIM), 1)
        for qb in range(tr // BLOCK):
            rows = slice(qb * BLOCK, (qb + 1) * BLOCK)
            prows = slice((qb - 1) * BLOCK, qb * BLOCK)
            m_t, l_t = m_ref[rows, :], l_ref[rows, :]
            for hh in range(HEADS_PER_STEP):
                if qb == 0:
                    kp, vp = halo_ref[hp, :, cols(0, hh)], halo_ref[hp, :, cols(1, hh)]
                    bias = jnp.where(first, bias_ref[0, hh], bias_ref[1, hh])
                else:
                    kp, vp = slab_ref[hp, prows, cols(1, hh)], slab_ref[hp, prows, cols(2, hh)]
                    bias = bias_ref[1, hh]
                o, m, l = _attend(slab_ref[hp, rows, cols(0, hh)], kp,
                                  slab_ref[hp, rows, cols(1, hh)], vp,
                                  slab_ref[hp, rows, cols(2, hh)], bias)
                o_ref[rows, hh * HEAD_DIM:(hh + 1) * HEAD_DIM] = o.astype(o_ref.dtype)
                head = hp * HEADS_PER_STEP + hh
                m_t = jnp.where(lane == head, m, m_t)
                l_t = jnp.where(lane == head, l, l_t)
            m_ref[rows, :] = m_t
            l_ref[rows, :] = l_t
        halo_ref[hp] = slab_ref[hp, tr - BLOCK:tr, pair:3 * pair]

    def projection():
        hn = hn_ref[...]
        for part, (w_ref, scale) in enumerate(((wq_ref, Q_SCALE), (wk_ref, None), (wv_ref, None))):
            slab_ref[hp, :, part * pair:(part + 1) * pair] = _project(
                hn, w_ref, scale).astype(slab_ref.dtype)

    @pl.when(i == 0)
    def _():
        projection()

    @pl.when((i > 0) & (i < last))
    def _():
        attention()
        projection()

    @pl.when(i == last)
    def _():
        attention()


def _proj_attention(hn, w, g, dil, seq):
    t = hn.shape[0]
    l_sub = seq // dil
    tr = min(ATTN_ROWS, l_sub)
    n_tiles = t // tr
    pair = HEADS_PER_STEP * HEAD_DIM
    n_pairs = N_HEADS // HEADS_PER_STEP
    bias = jnp.asarray(_attn_bias_table(g, dil))

    def tile_done(i, hp):
        return (jnp.maximum(i - 1, 0), 0)

    def w_spec(part):
        first = (part * N_GROUPS + g) * n_pairs
        return pl.BlockSpec((D_MODEL, pair), lambda i, hp: (0, first + hp))

    kern = functools.partial(_proj_attn_kernel, tiles_per_seq=l_sub // tr)
    return pl.pallas_call(
        kern,
        grid=(n_tiles + 1, n_pairs),
        in_specs=[pl.BlockSpec((tr, D_MODEL), lambda i, hp: (jnp.minimum(i, n_tiles - 1), 0)),
                  w_spec(0), w_spec(1), w_spec(2),
                  pl.BlockSpec((2, HEADS_PER_STEP, BLOCK, 2 * BLOCK),
                               lambda i, hp: (0, hp, 0, 0))],
        out_specs=[pl.BlockSpec((tr, pair),
                                lambda i, hp: (jnp.maximum(i - 1, 0), jnp.where(i == 0, 0, hp))),
                   pl.BlockSpec((tr, HEAD_DIM), tile_done),
                   pl.BlockSpec((tr, HEAD_DIM), tile_done)],
        out_shape=[jax.ShapeDtypeStruct((t, ATTN_WIDTH), BF16),
                   jax.ShapeDtypeStruct((t, HEAD_DIM), F32),
                   jax.ShapeDtypeStruct((t, HEAD_DIM), F32)],
        scratch_shapes=[pltpu.VMEM((n_pairs, tr, 3 * pair), BF16),
                        pltpu.VMEM((n_pairs, BLOCK, 2 * pair), BF16)],
        compiler_params=_params("arbitrary", "arbitrary"),
        name=f"proj_attn{g}",
    )(hn, w, w, w, bias)


def _finish(y_ref, x_ref, wout_ref, gpost_ref, gnext_ref, perm_refs, xo_ref, hn_refs):
    out = jnp.dot(y_ref[...], wout_ref[...], preferred_element_type=F32)
    xn = x_ref[...] + _rms(out, gpost_ref[...])
    xo_ref[...] = xn
    if hn_refs:
        _emit_hn(_rms(xn, gnext_ref[...]), perm_refs, hn_refs)


def _zero_on_first_step(y_ref):
    @pl.when(pl.program_id(0) == 0)
    def _():
        y_ref[...] = jnp.zeros_like(y_ref)


def _attn_out_kernel(o0_ref, o1_ref, o2_ref, m0_ref, m1_ref, m2_ref, l0_ref, l1_ref, l2_ref,
                     hn_ref, x_ref, wgate_ref, wout_ref, gpost_ref, gnext_ref, *refs):
    perm_refs, xo_ref, hn_refs, y_ref = refs[:4], refs[4], refs[5:-1], refs[-1]
    tm = x_ref.shape[0]
    _zero_on_first_step(y_ref)
    _finish(y_ref, x_ref, wout_ref, gpost_ref, gnext_ref, perm_refs, xo_ref, hn_refs)
    ms = [m0_ref[...], m1_ref[...], m2_ref[...]]
    m = jnp.maximum(jnp.maximum(ms[0], ms[1]), ms[2])
    es = [jnp.exp2(t - m) for t in ms]
    den = es[0] * l0_ref[...] + es[1] * l1_ref[...] + es[2] * l2_ref[...]
    ws = [e / den for e in es]
    hn = hn_ref[...]
    pair = MERGE_HEADS * HEAD_DIM
    for hp in range(N_HEADS // MERGE_HEADS):
        ps = slice(hp * pair, (hp + 1) * pair)
        gate = jnp.dot(hn, wgate_ref[:, ps], preferred_element_type=F32)
        os_ = [o0_ref[:, ps].astype(F32)]
        for k, o_ref in enumerate((o1_ref, o2_ref)):
            res = o_ref[:, :, ps].reshape(tm, pair)
            os_.append(jnp.dot(perm_refs[2 * k][...], res, preferred_element_type=F32))
        for hh in range(MERGE_HEADS):
            h = hp * MERGE_HEADS + hh
            sub = slice(hh * HEAD_DIM, (hh + 1) * HEAD_DIM)
            acc = ws[0][:, h:h + 1] * os_[0][:, sub]
            for g in range(1, N_GROUPS):
                acc = acc + ws[g][:, h:h + 1] * os_[g][:, sub]
            y_ref[:, h * HEAD_DIM:(h + 1) * HEAD_DIM] = (
                acc * _silu(gate[:, sub])).astype(y_ref.dtype)


def _attn_epilogue(os_, stats, hn, x2, wgate, wout, gpost, gnext, tiles, emit):
    tm = tiles.tm
    assert tm == PERM_TOKENS
    perms, perm_specs = _perm_inputs(True)
    hn_specs, hn_shapes = tiles.hn_out(emit, 1)
    return pl.pallas_call(
        _attn_out_kernel,
        grid=(tiles.n + 1,),
        in_specs=[tiles.tok(ATTN_WIDTH, 0), tiles.res(ATTN_WIDTH, DILATIONS[1], 0),
                  tiles.res(ATTN_WIDTH, DILATIONS[2], 0)]
                 + [tiles.tok(HEAD_DIM, 0)] * (2 * N_GROUPS)
                 + [tiles.tok(D_MODEL, 0), tiles.tok(D_MODEL, 1),
                    _const_spec((D_MODEL, ATTN_WIDTH)),
                    _const_spec((ATTN_WIDTH, D_MODEL)), _const_spec((1, D_MODEL)),
                    _const_spec((1, D_MODEL))] + perm_specs,
        out_specs=[tiles.tok(D_MODEL, 1)] + hn_specs,
        out_shape=[jax.ShapeDtypeStruct(x2.shape, F32)] + hn_shapes,
        scratch_shapes=[pltpu.VMEM((tm, ATTN_WIDTH), BF16)],
        compiler_params=_params("arbitrary"),
        name="attn_epilogue",
    )(*os_, *stats, hn, x2, wgate, wout, gpost.reshape(1, -1), gnext.reshape(1, -1), *perms)


def _conv_mix_kernel(hn_ref, wh_ref, wb_ref, wc_ref, wg_ref, cw_ref, y_ref, halo_ref, *,
                     tiles_per_seq):
    i, j = pl.program_id(0), pl.program_id(1)
    tr = hn_ref.shape[0]

    @pl.when((i == 0) & (j == 0))
    def _():
        halo_ref[...] = jnp.zeros_like(halo_ref)

    hn = hn_ref[...]
    u = _project(hn, wc_ref) * _project(hn, wh_ref)
    first = (i % tiles_per_seq) == 0
    ue = jnp.concatenate([jnp.where(first, 0.0, halo_ref[j]), u], axis=0)
    um1 = ue[HALO_ROWS - 1:HALO_ROWS - 1 + tr]
    um2 = ue[HALO_ROWS - 2:HALO_ROWS - 2 + tr]
    halo_ref[j] = u[tr - HALO_ROWS:tr]
    cw = cw_ref[...]
    conv = cw[0:1] * um2 + cw[1:2] * um1 + cw[2:3] * u
    y = _project(hn, wb_ref) * conv * _silu(_project(hn, wg_ref))
    y_ref[...] = y.astype(y_ref.dtype)


def _conv_mix(hn, w, conv_w, seq):
    t = hn.shape[0]
    tr = min(PROJ_ROWS, seq)
    n_chunks = D_MODEL // CONV_COLS

    def w_spec(part):
        return pl.BlockSpec((D_MODEL, CONV_COLS), lambda i, j: (0, part * n_chunks + j))

    kern = functools.partial(_conv_mix_kernel, tiles_per_seq=seq // tr)
    return pl.pallas_call(
        kern,
        grid=(t // tr, n_chunks),
        in_specs=[pl.BlockSpec((tr, D_MODEL), lambda i, j: (i, 0)),
                  w_spec(0), w_spec(1), w_spec(2), w_spec(3),
                  pl.BlockSpec((CONV_K, CONV_COLS), lambda i, j: (0, j))],
        out_specs=pl.BlockSpec((tr, CONV_COLS), lambda i, j: (i, j)),
        out_shape=jax.ShapeDtypeStruct((t, D_MODEL), BF16),
        scratch_shapes=[pltpu.VMEM((n_chunks, HALO_ROWS, CONV_COLS), F32)],
        compiler_params=_params("arbitrary", "arbitrary"),
        name="conv_mix",
    )(hn, w, w, w, w, conv_w)


def _out_kernel(y_ref, x_ref, wout_ref, gpost_ref, gnext_ref, *refs, n_perm):
    perm_refs, xo_ref, hn_refs = refs[:n_perm], refs[n_perm], refs[n_perm + 1:]
    _finish(y_ref, x_ref, wout_ref, gpost_ref, gnext_ref, perm_refs, xo_ref, hn_refs)


def _out_epilogue(y, x2, wout, gpost, gnext, tiles, emit):
    perms, perm_specs = _perm_inputs(emit >= 2)
    hn_specs, hn_shapes = tiles.hn_out(emit, 0)
    kern = functools.partial(_out_kernel, n_perm=len(perms))
    return pl.pallas_call(
        kern,
        grid=(tiles.n,),
        in_specs=[tiles.tok(D_MODEL, 0), tiles.tok(D_MODEL, 0),
                  _const_spec((D_MODEL, D_MODEL)), _const_spec((1, D_MODEL)),
                  _const_spec((1, D_MODEL))] + perm_specs,
        out_specs=[tiles.tok(D_MODEL, 0)] + hn_specs,
        out_shape=[jax.ShapeDtypeStruct(x2.shape, F32)] + hn_shapes,
        compiler_params=_params("parallel"),
        name="out_epilogue",
    )(y, x2, wout, gpost.reshape(1, -1), gnext.reshape(1, -1), *perms)


def _attention_layer(x2, hns, w_in, w_out, gpost, gnext, tiles, emit):
    t = x2.shape[0]
    batch, seq = tiles.batch, tiles.seq
    os_, ms, ls = [], [], []
    for g, dil in enumerate(DILATIONS):
        o, m, l = _proj_attention(hns[g].reshape(t, D_MODEL), w_in, g, dil, seq)
        os_.append(o if dil == 1 else o.reshape(batch, dil, seq // dil, ATTN_WIDTH))

        def to_token_order(a, dil=dil):
            a = a.reshape(batch, dil, seq // dil, HEAD_DIM)
            return jnp.swapaxes(a, 1, 2).reshape(t, HEAD_DIM)

        ms.append(to_token_order(m))
        ls.append(to_token_order(l))
    w_gate = w_in[:, 3 * N_GROUPS * ATTN_WIDTH:].astype(BF16)
    return _attn_epilogue(os_, ms + ls, hns[0], x2, w_gate, w_out.astype(BF16), gpost, gnext,
                          tiles, emit)


def _conv_layer(x2, hn, w_in, conv_w, w_out, gpost, gnext, tiles, emit):
    y = _conv_mix(hn, w_in, conv_w, tiles.seq)
    out_tiles = _Tiles(tiles.batch, tiles.seq, OUT_TOKENS)
    return _out_epilogue(y, x2, w_out.astype(BF16), gpost, gnext, out_tiles, emit)


def kernel(x, l0_norm_pre, l0_w_in, l0_w_out, l0_norm_post, l1_norm_pre, l1_w_in, l1_conv_w, l1_w_out, l1_norm_post, l2_norm_pre, l2_w_in, l2_w_out, l2_norm_post, l3_norm_pre, l3_w_in, l3_conv_w, l3_w_out, l3_norm_post):
    batch, seq, d = x.shape
    assert d == D_MODEL and seq % (DILATIONS[-1] * BLOCK) == 0
    tiles = _Tiles(batch, seq, EPILOGUE_TOKENS)
    x2 = x.reshape(batch * seq, D_MODEL)
    hns = _prenorm(x2, l0_norm_pre, tiles)
    x2, hn = _attention_layer(x2, hns, l0_w_in, l0_w_out, l0_norm_post, l1_norm_pre, tiles, 1)
    x2, *hns = _conv_layer(x2, hn, l1_w_in, l1_conv_w, l1_w_out, l1_norm_post, l2_norm_pre, tiles, 2)
    x2, hn = _attention_layer(x2, hns, l2_w_in, l2_w_out, l2_norm_post, l3_norm_pre, tiles, 1)
    (x2,) = _conv_layer(x2, hn, l3_w_in, l3_conv_w, l3_w_out, l3_norm_post, l3_norm_pre, tiles, 0)
    return x2.reshape(batch, seq, D_MODEL)
```

```python
import functools

import jax
import jax.numpy as jnp
import numpy as np
from jax import lax
from jax.experimental import pallas as pl
from jax.experimental.pallas import tpu as pltpu

D_MODEL = 2048
HEAD_DIM = 128
N_HEADS = 16
ATTN_WIDTH = N_HEADS * HEAD_DIM
DILATIONS = (1, 4, 16)
N_GROUPS = len(DILATIONS)
BLOCK = 128
CONV_K = 3
NORM_EPS = 1e-6
HALO_ROWS = 8
V7X_VMEM_LIMIT_BYTES = 56 * 1024 * 1024
MXU_COLS = 256
PERM_TOKENS = 256
EPILOGUE_TOKENS = 2 * PERM_TOKENS
PROJ_ROWS = 1024
CONV_COLS = 512
ATTN_ROWS = 1024
HEADS_PER_STEP = 4
MERGE_HEADS = 4
LOG2E = 1.4426950408889634
Q_SCALE = HEAD_DIM ** -0.5 * LOG2E

BF16 = jnp.bfloat16
F32 = jnp.float32


def _alibi_slopes():
    n = N_GROUPS * N_HEADS
    s = 2.0 ** (-8.0 * np.arange(1, n + 1) / n)
    return s.reshape(N_GROUPS, N_HEADS).astype(np.float32)


def _to_token_order(tm, dil):
    p = np.zeros((tm, tm), np.float32)
    u, r = np.meshgrid(np.arange(tm // dil), np.arange(dil), indexing="ij")
    p[(u * dil + r).ravel(), (r * (tm // dil) + u).ravel()] = 1.0
    return p


def _params(*semantics):
    return pltpu.CompilerParams(dimension_semantics=semantics,
                                vmem_limit_bytes=V7X_VMEM_LIMIT_BYTES)


def _rms(v, g):
    return v * lax.rsqrt(jnp.mean(v * v, axis=-1, keepdims=True) + NORM_EPS) * g


def _silu(g):
    half = 0.5 * g
    return half + half * jnp.tanh(half)


class _Tiles:
    def __init__(self, batch, seq):
        self.batch, self.seq, self.tm = batch, seq, EPILOGUE_TOKENS
        self.per_seq = seq // self.tm
        self.n = batch * self.per_seq

    def tok(self, width):
        return pl.BlockSpec((self.tm, width), lambda s: (s, 0))

    def res(self, width, dil):
        return pl.BlockSpec((None, dil, self.tm // dil, width),
                            lambda s: (s // self.per_seq, 0, s % self.per_seq, 0))

    def hn_out(self, emit):
        specs, shapes = [], []
        if emit >= 1:
            specs.append(self.tok(D_MODEL))
            shapes.append(jax.ShapeDtypeStruct((self.batch * self.seq, D_MODEL), BF16))
        if emit >= 2:
            for dil in DILATIONS[1:]:
                specs.append(self.res(D_MODEL, dil))
                shapes.append(jax.ShapeDtypeStruct(
                    (self.batch, dil, self.seq // dil, D_MODEL), BF16))
        return specs, shapes


def _const_spec(shape):
    return pl.BlockSpec(shape, lambda s: (0,) * len(shape), pipeline_mode=pl.Buffered(1))


def _to_residue_inputs(wanted):
    if not wanted:
        return [], []
    mats = [jnp.asarray(_to_token_order(PERM_TOKENS, dil).T, BF16) for dil in DILATIONS[1:]]
    return mats, [_const_spec((PERM_TOKENS, PERM_TOKENS))] * len(mats)


def _emit_hn(hn, perm_refs, out_refs):
    if not out_refs:
        return
    hb = hn.astype(BF16)
    out_refs[0][...] = hb
    for perm_ref, o_ref in zip(perm_refs, out_refs[1:]):
        to_res = perm_ref[...]
        dil = o_ref.shape[0]
        rows = PERM_TOKENS // dil
        for t0 in range(0, hb.shape[0], PERM_TOKENS):
            res = jnp.dot(to_res, hb[t0:t0 + PERM_TOKENS],
                          preferred_element_type=F32).astype(BF16)
            for r in range(dil):
                o_ref[r, t0 // dil:t0 // dil + rows] = res[r * rows:(r + 1) * rows]


def _norm_kernel(x_ref, g_ref, *refs):
    n_perm = N_GROUPS - 1
    _emit_hn(_rms(x_ref[...], g_ref[...]), refs[:n_perm], refs[n_perm:])


def _prenorm(x2, g, tiles):
    perms, perm_specs = _to_residue_inputs(True)
    out_specs, out_shapes = tiles.hn_out(2)
    return pl.pallas_call(
        _norm_kernel,
        grid=(tiles.n,),
        in_specs=[tiles.tok(D_MODEL), _const_spec((1, D_MODEL))] + perm_specs,
        out_specs=out_specs,
        out_shape=out_shapes,
        compiler_params=_params("parallel"),
        name="prenorm",
    )(x2, g.reshape(1, D_MODEL), *perms)


def _project(hn, w_ref, scale=None, cols=slice(None)):
    w = w_ref[:, cols] if scale is None else w_ref[:, cols] * scale
    return jnp.dot(hn, w.astype(BF16), preferred_element_type=F32)


def _attn_bias_table(g, dil):
    qi = np.arange(BLOCK)[:, None]
    kj = np.arange(2 * BLOCK)[None, :]
    dist = qi + BLOCK - kj
    in_window = (dist >= 0) & (dist <= BLOCK)
    slopes = _alibi_slopes()[g].astype(np.float64) * LOG2E
    bias = -(slopes[:, None, None] * (dil * dist)[None].astype(np.float64))
    general = np.where(in_window[None], bias, -np.inf)
    first = np.where((kj >= BLOCK)[None], general, -np.inf)
    return np.stack([first, general]).astype(np.float32)


def _attend(q, kp, kc, vp, vc, bias):
    k = jnp.concatenate([kp, kc], axis=0)
    v = jnp.concatenate([vp, vc], axis=0)
    s = lax.dot_general(q, k, (((1,), (1,)), ((), ())), preferred_element_type=F32) + bias
    m = jnp.max(s, axis=-1, keepdims=True)
    p = jnp.exp2(s - m)
    l = jnp.sum(p, axis=-1, keepdims=True)
    return jnp.dot(p.astype(BF16), v, preferred_element_type=F32), m, l


def _proj_attn_kernel(hn_ref, wq_ref, wk_ref, wv_ref, bias_ref, o_ref, m_ref, l_ref, slab_ref,
                      halo_ref, *, tiles_per_seq):
    i, hp = pl.program_id(0), pl.program_id(1)
    last = pl.num_programs(0) - 1
    tr = hn_ref.shape[0]
    pair = HEADS_PER_STEP * HEAD_DIM

    @pl.when((i == 0) & (hp == 0))
    def _():
        halo_ref[...] = jnp.zeros_like(halo_ref)

    @pl.when(hp == 0)
    def _():
        m_ref[...] = jnp.zeros_like(m_ref)
        l_ref[...] = jnp.ones_like(l_ref)

    def cols(part, hh):
        return slice(part * pair + hh * HEAD_DIM, part * pair + (hh + 1) * HEAD_DIM)

    def attention():
        first = ((i - 1) % tiles_per_seq) == 0
        lane = lax.broadcasted_iota(jnp.int32, (BLOCK, HEAD_DIM), 1)
        for qb in range(tr // BLOCK):
            rows = slice(qb * BLOCK, (qb + 1) * BLOCK)
            prows = slice((qb - 1) * BLOCK, qb * BLOCK)
            m_t, l_t = m_ref[rows, :], l_ref[rows, :]
            for hh in range(HEADS_PER_STEP):
                if qb == 0:
                    kp, vp = halo_ref[hp, :, cols(0, hh)], halo_ref[hp, :, cols(1, hh)]
                    bias = jnp.where(first, bias_ref[0, hh], bias_ref[1, hh])
                else:
                    kp, vp = slab_ref[hp, prows, cols(1, hh)], slab_ref[hp, prows, cols(2, hh)]
                    bias = bias_ref[1, hh]
                o, m, l = _attend(slab_ref[hp, rows, cols(0, hh)], kp,
                                  slab_ref[hp, rows, cols(1, hh)], vp,
                                  slab_ref[hp, rows, cols(2, hh)], bias)
                o_ref[rows, hh * HEAD_DIM:(hh + 1) * HEAD_DIM] = o.astype(o_ref.dtype)
                head = hp * HEADS_PER_STEP + hh
                m_t = jnp.where(lane == head, m, m_t)
                l_t = jnp.where(lane == head, l, l_t)
            m_ref[rows, :] = m_t
            l_ref[rows, :] = l_t
        halo_ref[hp] = slab_ref[hp, tr - BLOCK:tr, pair:3 * pair]

    def projection():
        hn = hn_ref[...]
        for part, (w_ref, scale) in enumerate(((wq_ref, Q_SCALE), (wk_ref, None), (wv_ref, None))):
            slab_ref[hp, :, part * pair:(part + 1) * pair] = _project(
                hn, w_ref, scale).astype(slab_ref.dtype)

    @pl.when(i == 0)
    def _():
        projection()

    @pl.when((i > 0) & (i < last))
    def _():
        attention()
        projection()

    @pl.when(i == last)
    def _():
        attention()


def _proj_attention(hn, w, g, dil, seq):
    t = hn.shape[0]
    l_sub = seq // dil
    tr = min(ATTN_ROWS, l_sub)
    n_tiles = t // tr
    pair = HEADS_PER_STEP * HEAD_DIM
    n_pairs = N_HEADS // HEADS_PER_STEP
    bias = jnp.asarray(_attn_bias_table(g, dil))

    def tile_done(i, hp):
        return (jnp.maximum(i - 1, 0), 0)

    def w_spec(part):
        first = (part * N_GROUPS + g) * n_pairs
        return pl.BlockSpec((D_MODEL, pair), lambda i, hp: (0, first + hp))

    kern = functools.partial(_proj_attn_kernel, tiles_per_seq=l_sub // tr)
    return pl.pallas_call(
        kern,
        grid=(n_tiles + 1, n_pairs),
        in_specs=[pl.BlockSpec((tr, D_MODEL), lambda i, hp: (jnp.minimum(i, n_tiles - 1), 0)),
                  w_spec(0), w_spec(1), w_spec(2),
                  pl.BlockSpec((2, HEADS_PER_STEP, BLOCK, 2 * BLOCK),
                               lambda i, hp: (0, hp, 0, 0))],
        out_specs=[pl.BlockSpec((tr, pair),
                                lambda i, hp: (jnp.maximum(i - 1, 0), jnp.where(i == 0, 0, hp))),
                   pl.BlockSpec((tr, HEAD_DIM), tile_done),
                   pl.BlockSpec((tr, HEAD_DIM), tile_done)],
        out_shape=[jax.ShapeDtypeStruct((t, ATTN_WIDTH), BF16),
                   jax.ShapeDtypeStruct((t, HEAD_DIM), F32),
                   jax.ShapeDtypeStruct((t, HEAD_DIM), F32)],
        scratch_shapes=[pltpu.VMEM((n_pairs, tr, 3 * pair), BF16),
                        pltpu.VMEM((n_pairs, BLOCK, 2 * pair), BF16)],
        compiler_params=_params("arbitrary", "arbitrary"),
        name=f"proj_attn{g}",
    )(hn, w, w, w, bias)


def _finish(y_ref, x_ref, wout_ref, gpost_ref, gnext_ref, perm_refs, xo_ref, hn_refs):
    out = jnp.dot(y_ref[...], wout_ref[...], preferred_element_type=F32)
    xn = x_ref[...] + _rms(out, gpost_ref[...])
    xo_ref[...] = xn
    if hn_refs:
        _emit_hn(_rms(xn, gnext_ref[...]), perm_refs, hn_refs)


def _gate_merge_kernel(o0_ref, o1_ref, o2_ref, m0_ref, m1_ref, m2_ref, l0_ref, l1_ref, l2_ref,
                       hn_ref, wgate_ref, p4_ref, p16_ref, y_ref, acc_ref):
    hq = pl.program_id(1)
    tr, width = y_ref.shape
    ms = [m0_ref[...], m1_ref[...], m2_ref[...]]
    m = jnp.maximum(jnp.maximum(ms[0], ms[1]), ms[2])
    es = [jnp.exp2(t - m) for t in ms]
    den = es[0] * l0_ref[...] + es[1] * l1_ref[...] + es[2] * l2_ref[...]
    to_front = lax.rem(HEAD_DIM - hq * MERGE_HEADS, HEAD_DIM)
    ws = [pltpu.roll(e / den, to_front, 1) for e in es]
    for t0 in range(0, tr, PERM_TOKENS):
        rows = slice(t0, t0 + PERM_TOKENS)
        os_ = [o0_ref[rows, :].astype(F32)]
        for o_ref, p_ref in ((o1_ref, p4_ref), (o2_ref, p16_ref)):
            n = PERM_TOKENS // o_ref.shape[0]
            res = o_ref[:, t0 // o_ref.shape[0]:t0 // o_ref.shape[0] + n, :]
            os_.append(jnp.dot(p_ref[...], res.reshape(PERM_TOKENS, width),
                               preferred_element_type=F32))
        for hh in range(MERGE_HEADS):
            sub = slice(hh * HEAD_DIM, (hh + 1) * HEAD_DIM)
            acc = ws[0][rows, hh:hh + 1] * os_[0][:, sub]
            for g in range(1, N_GROUPS):
                acc = acc + ws[g][rows, hh:hh + 1] * os_[g][:, sub]
            acc_ref[rows, sub] = acc
    hn = hn_ref[...]
    for c0 in range(0, width, MXU_COLS):
        cols = slice(c0, c0 + MXU_COLS)
        gate = _project(hn, wgate_ref, cols=cols)
        y_ref[:, cols] = (acc_ref[:, cols] * _silu(gate)).astype(y_ref.dtype)


def _gate_merge(os_, stats, hn, w_in, batch, seq):
    t = hn.shape[0]
    tr = min(PROJ_ROWS, seq)
    width = MERGE_HEADS * HEAD_DIM
    n_steps = ATTN_WIDTH // width
    per_seq = seq // tr
    gate0 = 3 * N_GROUPS * n_steps

    def res_spec(dil):
        return pl.BlockSpec((None, dil, tr // dil, width),
                            lambda i, hq: (i // per_seq, 0, i % per_seq, hq))

    tok = pl.BlockSpec((tr, width), lambda i, hq: (i, hq))
    stat = pl.BlockSpec((tr, HEAD_DIM), lambda i, hq: (i, 0))
    perm = pl.BlockSpec((PERM_TOKENS, PERM_TOKENS), lambda i, hq: (0, 0),
                        pipeline_mode=pl.Buffered(1))
    to_token = [jnp.asarray(_to_token_order(PERM_TOKENS, dil), BF16) for dil in DILATIONS[1:]]
    return pl.pallas_call(
        _gate_merge_kernel,
        grid=(t // tr, n_steps),
        in_specs=[tok, res_spec(DILATIONS[1]), res_spec(DILATIONS[2])] + [stat] * (2 * N_GROUPS)
                 + [pl.BlockSpec((tr, D_MODEL), lambda i, hq: (i, 0)),
                    pl.BlockSpec((D_MODEL, width), lambda i, hq: (0, gate0 + hq)), perm, perm],
        out_specs=tok,
        out_shape=jax.ShapeDtypeStruct((t, ATTN_WIDTH), BF16),
        scratch_shapes=[pltpu.VMEM((tr, width), F32)],
        compiler_params=_params("parallel", "arbitrary"),
        name="gate_merge",
    )(*os_, *stats, hn, w_in, *to_token)


def _conv_mix_kernel(hn_ref, wh_ref, wb_ref, wc_ref, wg_ref, cw_ref, y_ref, halo_ref, *,
                     tiles_per_seq):
    i, j = pl.program_id(0), pl.program_id(1)
    tr = hn_ref.shape[0]

    @pl.when((i == 0) & (j == 0))
    def _():
        halo_ref[...] = jnp.zeros_like(halo_ref)

    hn = hn_ref[...]
    u = _project(hn, wc_ref) * _project(hn, wh_ref)
    first = (i % tiles_per_seq) == 0
    ue = jnp.concatenate([jnp.where(first, 0.0, halo_ref[j]), u], axis=0)
    um1 = ue[HALO_ROWS - 1:HALO_ROWS - 1 + tr]
    um2 = ue[HALO_ROWS - 2:HALO_ROWS - 2 + tr]
    halo_ref[j] = u[tr - HALO_ROWS:tr]
    cw = cw_ref[...]
    conv = cw[0:1] * um2 + cw[1:2] * um1 + cw[2:3] * u
    y = _project(hn, wb_ref) * conv * _silu(_project(hn, wg_ref))
    y_ref[...] = y.astype(y_ref.dtype)


def _conv_mix(hn, w, conv_w, seq):
    t = hn.shape[0]
    tr = min(PROJ_ROWS, seq)
    n_chunks = D_MODEL // CONV_COLS

    def w_spec(part):
        return pl.BlockSpec((D_MODEL, CONV_COLS), lambda i, j: (0, part * n_chunks + j))

    kern = functools.partial(_conv_mix_kernel, tiles_per_seq=seq // tr)
    return pl.pallas_call(
        kern,
        grid=(t // tr, n_chunks),
        in_specs=[pl.BlockSpec((tr, D_MODEL), lambda i, j: (i, 0)),
                  w_spec(0), w_spec(1), w_spec(2), w_spec(3),
                  pl.BlockSpec((CONV_K, CONV_COLS), lambda i, j: (0, j))],
        out_specs=pl.BlockSpec((tr, CONV_COLS), lambda i, j: (i, j)),
        out_shape=jax.ShapeDtypeStruct((t, D_MODEL), BF16),
        scratch_shapes=[pltpu.VMEM((n_chunks, HALO_ROWS, CONV_COLS), F32)],
        compiler_params=_params("arbitrary", "arbitrary"),
        name="conv_mix",
    )(hn, w, w, w, w, conv_w)


def _out_kernel(y_ref, x_ref, wout_ref, gpost_ref, gnext_ref, *refs, n_perm):
    perm_refs, xo_ref, hn_refs = refs[:n_perm], refs[n_perm], refs[n_perm + 1:]
    _finish(y_ref, x_ref, wout_ref, gpost_ref, gnext_ref, perm_refs, xo_ref, hn_refs)


def _out_epilogue(y, x2, wout, gpost, gnext, tiles, emit):
    perms, perm_specs = _to_residue_inputs(emit >= 2)
    hn_specs, hn_shapes = tiles.hn_out(emit)
    kern = functools.partial(_out_kernel, n_perm=len(perms))
    return pl.pallas_call(
        kern,
        grid=(tiles.n,),
        in_specs=[tiles.tok(D_MODEL), tiles.tok(D_MODEL),
                  _const_spec((D_MODEL, D_MODEL)), _const_spec((1, D_MODEL)),
                  _const_spec((1, D_MODEL))] + perm_specs,
        out_specs=[tiles.tok(D_MODEL)] + hn_specs,
        out_shape=[jax.ShapeDtypeStruct(x2.shape, F32)] + hn_shapes,
        compiler_params=_params("parallel"),
        name="out_epilogue",
    )(y, x2, wout, gpost.reshape(1, -1), gnext.reshape(1, -1), *perms)


def _attention_layer(x2, hns, w_in, w_out, gpost, gnext, tiles, emit):
    t = x2.shape[0]
    batch, seq = tiles.batch, tiles.seq
    os_, ms, ls = [], [], []
    for g, dil in enumerate(DILATIONS):
        o, m, l = _proj_attention(hns[g].reshape(t, D_MODEL), w_in, g, dil, seq)
        os_.append(o if dil == 1 else o.reshape(batch, dil, seq // dil, ATTN_WIDTH))

        def to_token_order(a, dil=dil):
            a = a.reshape(batch, dil, seq // dil, HEAD_DIM)
            return jnp.swapaxes(a, 1, 2).reshape(t, HEAD_DIM)

        ms.append(to_token_order(m))
        ls.append(to_token_order(l))
    y = _gate_merge(os_, ms + ls, hns[0], w_in, batch, seq)
    return _out_epilogue(y, x2, w_out.astype(BF16), gpost, gnext, tiles, emit)


def _conv_layer(x2, hn, w_in, conv_w, w_out, gpost, gnext, tiles, emit):
    y = _conv_mix(hn, w_in, conv_w, tiles.seq)
    return _out_epilogue(y, x2, w_out.astype(BF16), gpost, gnext, tiles, emit)


def kernel(x, l0_norm_pre, l0_w_in, l0_w_out, l0_norm_post, l1_norm_pre, l1_w_in, l1_conv_w, l1_w_out, l1_norm_post, l2_norm_pre, l2_w_in, l2_w_out, l2_norm_post, l3_norm_pre, l3_w_in, l3_conv_w, l3_w_out, l3_norm_post):
    batch, seq, d = x.shape
    assert d == D_MODEL and seq % (DILATIONS[-1] * BLOCK) == 0
    tiles = _Tiles(batch, seq)
    x2 = x.reshape(batch * seq, D_MODEL)
    hns = _prenorm(x2, l0_norm_pre, tiles)
    x2, hn = _attention_layer(x2, hns, l0_w_in, l0_w_out, l0_norm_post, l1_norm_pre, tiles, 1)
    x2, *hns = _conv_layer(x2, hn, l1_w_in, l1_conv_w, l1_w_out, l1_norm_post, l2_norm_pre, tiles, 2)
    x2, hn = _attention_layer(x2, hns, l2_w_in, l2_w_out, l2_norm_post, l3_norm_pre, tiles, 1)
    (x2,) = _conv_layer(x2, hn, l3_w_in, l3_conv_w, l3_w_out, l3_norm_post, l3_norm_pre, tiles, 0)
    return x2.reshape(batch, seq, D_MODEL)
```

```python
import functools

import jax
import jax.numpy as jnp
import numpy as np
from jax import lax
from jax.experimental import pallas as pl
from jax.experimental.pallas import tpu as pltpu

D_MODEL = 2048
HEAD_DIM = 128
N_HEADS = 16
ATTN_WIDTH = N_HEADS * HEAD_DIM
DILATIONS = (1, 4, 16)
N_GROUPS = len(DILATIONS)
BLOCK = 128
CONV_K = 3
NORM_EPS = 1e-6
HALO_ROWS = 8
V7X_VMEM_LIMIT_BYTES = 56 * 1024 * 1024
MXU_COLS = 256
PERM_TOKENS = 256
EPILOGUE_TOKENS = PERM_TOKENS
OUT_TOKENS = 2 * PERM_TOKENS
PROJ_ROWS = 1024
CONV_COLS = 512
ATTN_ROWS = 1024
HEADS_PER_STEP = 4
MERGE_HEADS = 2
LOG2E = 1.4426950408889634
Q_SCALE = HEAD_DIM ** -0.5 * LOG2E

BF16 = jnp.bfloat16
F32 = jnp.float32


def _alibi_slopes():
    n = N_GROUPS * N_HEADS
    s = 2.0 ** (-8.0 * np.arange(1, n + 1) / n)
    return s.reshape(N_GROUPS, N_HEADS).astype(np.float32)


def _to_token_order(tm, dil):
    p = np.zeros((tm, tm), np.float32)
    u, r = np.meshgrid(np.arange(tm // dil), np.arange(dil), indexing="ij")
    p[(u * dil + r).ravel(), (r * (tm // dil) + u).ravel()] = 1.0
    return p


def _params(*semantics):
    return pltpu.CompilerParams(dimension_semantics=semantics,
                                vmem_limit_bytes=V7X_VMEM_LIMIT_BYTES)


def _rms(v, g):
    return v * lax.rsqrt(jnp.mean(v * v, axis=-1, keepdims=True) + NORM_EPS) * g


def _silu(g):
    half = 0.5 * g
    return half + half * jnp.tanh(half)


class _Tiles:
    def __init__(self, batch, seq, tm):
        self.batch, self.seq, self.tm = batch, seq, tm
        self.per_seq = seq // tm
        self.n = batch * self.per_seq

    def tile(self, s, lag):
        return jnp.clip(s - lag, 0, self.n - 1)

    def tok(self, width, lag, col=0):
        return pl.BlockSpec((self.tm, width), lambda s: (self.tile(s, lag), col))

    def res(self, width, dil, lag):
        def index(s):
            j = self.tile(s, lag)
            return (j // self.per_seq, 0, j % self.per_seq, 0)
        return pl.BlockSpec((None, dil, self.tm // dil, width), index)

    def hn_out(self, emit, lag):
        specs, shapes = [], []
        if emit >= 1:
            specs.append(self.tok(D_MODEL, lag))
            shapes.append(jax.ShapeDtypeStruct((self.batch * self.seq, D_MODEL), BF16))
        if emit >= 2:
            for dil in DILATIONS[1:]:
                specs.append(self.res(D_MODEL, dil, lag))
                shapes.append(jax.ShapeDtypeStruct(
                    (self.batch, dil, self.seq // dil, D_MODEL), BF16))
        return specs, shapes


def _const_spec(shape):
    return pl.BlockSpec(shape, lambda s: (0,) * len(shape), pipeline_mode=pl.Buffered(1))


def _perm_inputs(wanted):
    if not wanted:
        return [], []
    mats = []
    for dil in DILATIONS[1:]:
        p = _to_token_order(PERM_TOKENS, dil)
        mats += [p, p.T]
    specs = [_const_spec((PERM_TOKENS, PERM_TOKENS))] * len(mats)
    return [jnp.asarray(m, BF16) for m in mats], specs


def _emit_hn(hn, perm_refs, out_refs):
    if not out_refs:
        return
    hb = hn.astype(BF16)
    out_refs[0][...] = hb
    for k, o_ref in enumerate(out_refs[1:]):
        to_res = perm_refs[2 * k + 1][...]
        dil = o_ref.shape[0]
        rows = PERM_TOKENS // dil
        for t0 in range(0, hb.shape[0], PERM_TOKENS):
            res = jnp.dot(to_res, hb[t0:t0 + PERM_TOKENS],
                          preferred_element_type=F32).astype(BF16)
            for r in range(dil):
                o_ref[r, t0 // dil:t0 // dil + rows] = res[r * rows:(r + 1) * rows]


def _norm_kernel(x_ref, g_ref, *refs):
    perm_refs, out_refs = refs[:4], refs[4:]
    _emit_hn(_rms(x_ref[...], g_ref[...]), perm_refs, out_refs)


def _prenorm(x2, g, tiles):
    perms, perm_specs = _perm_inputs(True)
    out_specs, out_shapes = tiles.hn_out(2, 0)
    return pl.pallas_call(
        _norm_kernel,
        grid=(tiles.n,),
        in_specs=[tiles.tok(D_MODEL, 0), _const_spec((1, D_MODEL))] + perm_specs,
        out_specs=out_specs,
        out_shape=out_shapes,
        compiler_params=_params("parallel"),
        name="prenorm",
    )(x2, g.reshape(1, D_MODEL), *perms)


def _project(hn, w_ref, scale=None, cols=slice(None)):
    w = w_ref[:, cols] if scale is None else w_ref[:, cols] * scale
    return jnp.dot(hn, w.astype(BF16), preferred_element_type=F32)


def _attn_bias_table(g, dil):
    qi = np.arange(BLOCK)[:, None]
    kj = np.arange(2 * BLOCK)[None, :]
    dist = qi + BLOCK - kj
    in_window = (dist >= 0) & (dist <= BLOCK)
    slopes = _alibi_slopes()[g].astype(np.float64) * LOG2E
    bias = -(slopes[:, None, None] * (dil * dist)[None].astype(np.float64))
    general = np.where(in_window[None], bias, -np.inf)
    first = np.where((kj >= BLOCK)[None], general, -np.inf)
    return np.stack([first, general]).astype(np.float32)


def _attend(q, kp, kc, vp, vc, bias):
    k = jnp.concatenate([kp, kc], axis=0)
    v = jnp.concatenate([vp, vc], axis=0)
    s = lax.dot_general(q, k, (((1,), (1,)), ((), ())), preferred_element_type=F32) + bias
    m = jnp.max(s, axis=-1, keepdims=True)
    p = jnp.exp2(s - m)
    l = jnp.sum(p, axis=-1, keepdims=True)
    return jnp.dot(p.astype(BF16), v, preferred_element_type=F32), m, l


def _proj_attn_kernel(hn_ref, wq_ref, wk_ref, wv_ref, bias_ref, o_ref, m_ref, l_ref, slab_ref,
                      halo_ref, *, tiles_per_seq):
    i, hp = pl.program_id(0), pl.program_id(1)
    last = pl.num_programs(0) - 1
    tr = hn_ref.shape[0]
    pair = HEADS_PER_STEP * HEAD_DIM

    @pl.when((i == 0) & (hp == 0))
    def _():
        halo_ref[...] = jnp.zeros_like(halo_ref)

    @pl.when(hp == 0)
    def _():
        m_ref[...] = jnp.zeros_like(m_ref)
        l_ref[...] = jnp.ones_like(l_ref)

    def cols(part, hh):
        return slice(part * pair + hh * HEAD_DIM, part * pair + (hh + 1) * HEAD_DIM)

    def attention():
        first = ((i - 1) % tiles_per_seq) == 0
        lane = lax.broadcasted_iota(jnp.int32, (BLOCK, HEAD_DIM), 1)
        for qb in range(tr // BLOCK):
            rows = slice(qb * BLOCK, (qb + 1) * BLOCK)
            prows = slice((qb - 1) * BLOCK, qb * BLOCK)
            m_t, l_t = m_ref[rows, :], l_ref[rows, :]
            for hh in range(HEADS_PER_STEP):
                if qb == 0:
                    kp, vp = halo_ref[hp, :, cols(0, hh)], halo_ref[hp, :, cols(1, hh)]
                    bias = jnp.where(first, bias_ref[0, hh], bias_ref[1, hh])
                else:
                    kp, vp = slab_ref[hp, prows, cols(1, hh)], slab_ref[hp, prows, cols(2, hh)]
                    bias = bias_ref[1, hh]
                o, m, l = _attend(slab_ref[hp, rows, cols(0, hh)], kp,
                                  slab_ref[hp, rows, cols(1, hh)], vp,
                                  slab_ref[hp, rows, cols(2, hh)], bias)
                o_ref[rows, hh * HEAD_DIM:(hh + 1) * HEAD_DIM] = o.astype(o_ref.dtype)
                head = hp * HEADS_PER_STEP + hh
                m_t = jnp.where(lane == head, m, m_t)
                l_t = jnp.where(lane == head, l, l_t)
            m_ref[rows, :] = m_t
            l_ref[rows, :] = l_t
        halo_ref[hp] = slab_ref[hp, tr - BLOCK:tr, pair:3 * pair]

    def projection():
        hn = hn_ref[...]
        for part, (w_ref, scale) in enumerate(((wq_ref, Q_SCALE), (wk_ref, None), (wv_ref, None))):
            slab_ref[hp, :, part * pair:(part + 1) * pair] = _project(
                hn, w_ref, scale).astype(slab_ref.dtype)

    @pl.when(i == 0)
    def _():
        projection()

    @pl.when((i > 0) & (i < last))
    def _():
        attention()
        projection()

    @pl.when(i == last)
    def _():
        attention()


def _proj_attention(hn, w, g, dil, seq):
    t = hn.shape[0]
    l_sub = seq // dil
    tr = min(ATTN_ROWS, l_sub)
    n_tiles = t // tr
    pair = HEADS_PER_STEP * HEAD_DIM
    n_pairs = N_HEADS // HEADS_PER_STEP
    bias = jnp.asarray(_attn_bias_table(g, dil))

    def tile_done(i, hp):
        return (jnp.maximum(i - 1, 0), 0)

    def w_spec(part):
        first = (part * N_GROUPS + g) * n_pairs
        return pl.BlockSpec((D_MODEL, pair), lambda i, hp: (0, first + hp))

    kern = functools.partial(_proj_attn_kernel, tiles_per_seq=l_sub // tr)
    return pl.pallas_call(
        kern,
        grid=(n_tiles + 1, n_pairs),
        in_specs=[pl.BlockSpec((tr, D_MODEL), lambda i, hp: (jnp.minimum(i, n_tiles - 1), 0)),
                  w_spec(0), w_spec(1), w_spec(2),
                  pl.BlockSpec((2, HEADS_PER_STEP, BLOCK, 2 * BLOCK),
                               lambda i, hp: (0, hp, 0, 0))],
        out_specs=[pl.BlockSpec((tr, pair),
                                lambda i, hp: (jnp.maximum(i - 1, 0), jnp.where(i == 0, 0, hp))),
                   pl.BlockSpec((tr, HEAD_DIM), tile_done),
                   pl.BlockSpec((tr, HEAD_DIM), tile_done)],
        out_shape=[jax.ShapeDtypeStruct((t, ATTN_WIDTH), BF16),
                   jax.ShapeDtypeStruct((t, HEAD_DIM), F32),
                   jax.ShapeDtypeStruct((t, HEAD_DIM), F32)],
        scratch_shapes=[pltpu.VMEM((n_pairs, tr, 3 * pair), BF16),
                        pltpu.VMEM((n_pairs, BLOCK, 2 * pair), BF16)],
        compiler_params=_params("arbitrary", "arbitrary"),
        name=f"proj_attn{g}",
    )(hn, w, w, w, bias)


def _finish(y_ref, x_ref, wout_ref, gpost_ref, gnext_ref, perm_refs, xo_ref, hn_refs):
    out = jnp.dot(y_ref[...], wout_ref[...], preferred_element_type=F32)
    xn = x_ref[...] + _rms(out, gpost_ref[...])
    xo_ref[...] = xn
    if hn_refs:
        _emit_hn(_rms(xn, gnext_ref[...]), perm_refs, hn_refs)


def _attn_out_kernel(o0_ref, o1_ref, o2_ref, m0_ref, m1_ref, m2_ref, l0_ref, l1_ref, l2_ref,
                     hn_ref, x_ref, wgate_ref, wout_ref, gpost_ref, gnext_ref, *refs):
    perm_refs, xo_ref, hn_refs, y_ref = refs[:4], refs[4], refs[5:-1], refs[-1]
    tm = x_ref.shape[0]

    @pl.when(pl.program_id(0) == 0)
    def _():
        y_ref[...] = jnp.zeros_like(y_ref)

    _finish(y_ref, x_ref, wout_ref, gpost_ref, gnext_ref, perm_refs, xo_ref, hn_refs)
    ms = [m0_ref[...], m1_ref[...], m2_ref[...]]
    m = jnp.maximum(jnp.maximum(ms[0], ms[1]), ms[2])
    es = [jnp.exp2(t - m) for t in ms]
    den = es[0] * l0_ref[...] + es[1] * l1_ref[...] + es[2] * l2_ref[...]
    ws = [e / den for e in es]
    hn = hn_ref[...]
    pair = MERGE_HEADS * HEAD_DIM
    for hp in range(N_HEADS // MERGE_HEADS):
        ps = slice(hp * pair, (hp + 1) * pair)
        gate = jnp.dot(hn, wgate_ref[:, ps], preferred_element_type=F32)
        os_ = [o0_ref[:, ps].astype(F32)]
        for k, o_ref in enumerate((o1_ref, o2_ref)):
            res = o_ref[:, :, ps].reshape(tm, pair)
            os_.append(jnp.dot(perm_refs[2 * k][...], res, preferred_element_type=F32))
        for hh in range(MERGE_HEADS):
            h = hp * MERGE_HEADS + hh
            sub = slice(hh * HEAD_DIM, (hh + 1) * HEAD_DIM)
            acc = ws[0][:, h:h + 1] * os_[0][:, sub]
            for g in range(1, N_GROUPS):
                acc = acc + ws[g][:, h:h + 1] * os_[g][:, sub]
            y_ref[:, h * HEAD_DIM:(h + 1) * HEAD_DIM] = (
                acc * _silu(gate[:, sub])).astype(y_ref.dtype)


def _attn_epilogue(os_, stats, hn, x2, wgate, wout, gpost, gnext, tiles, emit):
    tm = tiles.tm
    assert tm == PERM_TOKENS
    perms, perm_specs = _perm_inputs(True)
    hn_specs, hn_shapes = tiles.hn_out(emit, 1)
    return pl.pallas_call(
        _attn_out_kernel,
        grid=(tiles.n + 1,),
        in_specs=[tiles.tok(ATTN_WIDTH, 0), tiles.res(ATTN_WIDTH, DILATIONS[1], 0),
                  tiles.res(ATTN_WIDTH, DILATIONS[2], 0)]
                 + [tiles.tok(HEAD_DIM, 0)] * (2 * N_GROUPS)
                 + [tiles.tok(D_MODEL, 0), tiles.tok(D_MODEL, 1),
                    _const_spec((D_MODEL, ATTN_WIDTH)),
                    _const_spec((ATTN_WIDTH, D_MODEL)), _const_spec((1, D_MODEL)),
                    _const_spec((1, D_MODEL))] + perm_specs,
        out_specs=[tiles.tok(D_MODEL, 1)] + hn_specs,
        out_shape=[jax.ShapeDtypeStruct(x2.shape, F32)] + hn_shapes,
        scratch_shapes=[pltpu.VMEM((tm, ATTN_WIDTH), BF16)],
        compiler_params=_params("arbitrary"),
        name="attn_epilogue",
    )(*os_, *stats, hn, x2, wgate, wout, gpost.reshape(1, -1), gnext.reshape(1, -1), *perms)


def _conv_mix_kernel(hn_ref, wh_ref, wb_ref, wc_ref, wg_ref, cw_ref, y_ref, halo_ref, *,
                     tiles_per_seq):
    i, j = pl.program_id(0), pl.program_id(1)
    tr = hn_ref.shape[0]

    @pl.when((i == 0) & (j == 0))
    def _():
        halo_ref[...] = jnp.zeros_like(halo_ref)

    hn = hn_ref[...]
    first = (i % tiles_per_seq) == 0
    for c0 in range(0, y_ref.shape[1], MXU_COLS):
        cs = slice(c0, c0 + MXU_COLS)
        u = _project(hn, wc_ref, cols=cs) * _project(hn, wh_ref, cols=cs)
        ue = jnp.concatenate([jnp.where(first, 0.0, halo_ref[j, :, cs]), u], axis=0)
        um1 = ue[HALO_ROWS - 1:HALO_ROWS - 1 + tr]
        um2 = ue[HALO_ROWS - 2:HALO_ROWS - 2 + tr]
        halo_ref[j, :, cs] = u[tr - HALO_ROWS:tr]
        cw = cw_ref[:, cs]
        conv = cw[0:1] * um2 + cw[1:2] * um1 + cw[2:3] * u
        y = _project(hn, wb_ref, cols=cs) * conv * _silu(_project(hn, wg_ref, cols=cs))
        y_ref[:, cs] = y.astype(y_ref.dtype)


def _conv_mix(hn, w, conv_w, seq):
    t = hn.shape[0]
    tr = min(PROJ_ROWS, seq)
    n_chunks = D_MODEL // CONV_COLS

    def w_spec(part):
        return pl.BlockSpec((D_MODEL, CONV_COLS), lambda i, j: (0, part * n_chunks + j))

    kern = functools.partial(_conv_mix_kernel, tiles_per_seq=seq // tr)
    return pl.pallas_call(
        kern,
        grid=(t // tr, n_chunks),
        in_specs=[pl.BlockSpec((tr, D_MODEL), lambda i, j: (i, 0)),
                  w_spec(0), w_spec(1), w_spec(2), w_spec(3),
                  pl.BlockSpec((CONV_K, CONV_COLS), lambda i, j: (0, j))],
        out_specs=pl.BlockSpec((tr, CONV_COLS), lambda i, j: (i, j)),
        out_shape=jax.ShapeDtypeStruct((t, D_MODEL), BF16),
        scratch_shapes=[pltpu.VMEM((n_chunks, HALO_ROWS, CONV_COLS), F32)],
        compiler_params=_params("arbitrary", "arbitrary"),
        name="conv_mix",
    )(hn, w, w, w, w, conv_w)


def _out_kernel(y_ref, x_ref, wout_ref, gpost_ref, gnext_ref, *refs, n_perm):
    perm_refs, xo_ref, hn_refs = refs[:n_perm], refs[n_perm], refs[n_perm + 1:]
    _finish(y_ref, x_ref, wout_ref, gpost_ref, gnext_ref, perm_refs, xo_ref, hn_refs)


def _out_epilogue(y, x2, wout, gpost, gnext, tiles, emit):
    perms, perm_specs = _perm_inputs(emit >= 2)
    hn_specs, hn_shapes = tiles.hn_out(emit, 0)
    kern = functools.partial(_out_kernel, n_perm=len(perms))
    return pl.pallas_call(
        kern,
        grid=(tiles.n,),
        in_specs=[tiles.tok(D_MODEL, 0), tiles.tok(D_MODEL, 0),
                  _const_spec((D_MODEL, D_MODEL)), _const_spec((1, D_MODEL)),
                  _const_spec((1, D_MODEL))] + perm_specs,
        out_specs=[tiles.tok(D_MODEL, 0)] + hn_specs,
        out_shape=[jax.ShapeDtypeStruct(x2.shape, F32)] + hn_shapes,
        compiler_params=_params("parallel"),
        name="out_epilogue",
    )(y, x2, wout, gpost.reshape(1, -1), gnext.reshape(1, -1), *perms)


def _attention_layer(x2, hns, w_in, w_out, gpost, gnext, tiles, emit):
    t = x2.shape[0]
    batch, seq = tiles.batch, tiles.seq
    os_, ms, ls = [], [], []
    for g, dil in enumerate(DILATIONS):
        o, m, l = _proj_attention(hns[g].reshape(t, D_MODEL), w_in, g, dil, seq)
        os_.append(o if dil == 1 else o.reshape(batch, dil, seq // dil, ATTN_WIDTH))

        def to_token_order(a, dil=dil):
            a = a.reshape(batch, dil, seq // dil, HEAD_DIM)
            return jnp.swapaxes(a, 1, 2).reshape(t, HEAD_DIM)

        ms.append(to_token_order(m))
        ls.append(to_token_order(l))
    w_gate = w_in[:, 3 * N_GROUPS * ATTN_WIDTH:].astype(BF16)
    return _attn_epilogue(os_, ms + ls, hns[0], x2, w_gate, w_out.astype(BF16), gpost, gnext,
                          tiles, emit)


def _conv_layer(x2, hn, w_in, conv_w, w_out, gpost, gnext, tiles, emit):
    y = _conv_mix(hn, w_in, conv_w, tiles.seq)
    out_tiles = _Tiles(tiles.batch, tiles.seq, OUT_TOKENS)
    return _out_epilogue(y, x2, w_out.astype(BF16), gpost, gnext, out_tiles, emit)


def kernel(x, l0_norm_pre, l0_w_in, l0_w_out, l0_norm_post, l1_norm_pre, l1_w_in, l1_conv_w, l1_w_out, l1_norm_post, l2_norm_pre, l2_w_in, l2_w_out, l2_norm_post, l3_norm_pre, l3_w_in, l3_conv_w, l3_w_out, l3_norm_post):
    batch, seq, d = x.shape
    assert d == D_MODEL and seq % (DILATIONS[-1] * BLOCK) == 0
    tiles = _Tiles(batch, seq, EPILOGUE_TOKENS)
    x2 = x.reshape(batch * seq, D_MODEL)
    hns = _prenorm(x2, l0_norm_pre, _Tiles(batch, seq, OUT_TOKENS))
    x2, hn = _attention_layer(x2, hns, l0_w_in, l0_w_out, l0_norm_post, l1_norm_pre, tiles, 1)
    x2, *hns = _conv_layer(x2, hn, l1_w_in, l1_conv_w, l1_w_out, l1_norm_post, l2_norm_pre, tiles, 2)
    x2, hn = _attention_layer(x2, hns, l2_w_in, l2_w_out, l2_norm_post, l3_norm_pre, tiles, 1)
    (x2,) = _conv_layer(x2, hn, l3_w_in, l3_conv_w, l3_w_out, l3_norm_post, l3_norm_pre, tiles, 0)
    return x2.reshape(batch, seq, D_MODEL)
```

```python
import functools

import jax
import jax.numpy as jnp
import numpy as np
from jax import lax
from jax.experimental import pallas as pl
from jax.experimental.pallas import tpu as pltpu

D_MODEL = 2048
HEAD_DIM = 128
N_HEADS = 16
ATTN_WIDTH = N_HEADS * HEAD_DIM
DILATIONS = (1, 4, 16)
N_GROUPS = len(DILATIONS)
BLOCK = 128
CONV_K = 3
NORM_EPS = 1e-6
HALO_ROWS = 8
V7X_VMEM_LIMIT_BYTES = 56 * 1024 * 1024
MXU_COLS = 256
PERM_TOKENS = 256
EPILOGUE_TOKENS = PERM_TOKENS
OUT_TOKENS = 2 * PERM_TOKENS
PROJ_ROWS = 1024
CONV_COLS = 512
ATTN_ROWS = 1024
HEADS_PER_STEP = 4
MERGE_HEADS = 2
LOG2E = 1.4426950408889634
Q_SCALE = HEAD_DIM ** -0.5 * LOG2E

BF16 = jnp.bfloat16
F32 = jnp.float32


def _alibi_slopes():
    n = N_GROUPS * N_HEADS
    s = 2.0 ** (-8.0 * np.arange(1, n + 1) / n)
    return s.reshape(N_GROUPS, N_HEADS).astype(np.float32)


def _to_token_order(tm, dil):
    p = np.zeros((tm, tm), np.float32)
    u, r = np.meshgrid(np.arange(tm // dil), np.arange(dil), indexing="ij")
    p[(u * dil + r).ravel(), (r * (tm // dil) + u).ravel()] = 1.0
    return p


def _params(*semantics):
    return pltpu.CompilerParams(dimension_semantics=semantics,
                                vmem_limit_bytes=V7X_VMEM_LIMIT_BYTES)


def _rms(v, g):
    return v * lax.rsqrt(jnp.mean(v * v, axis=-1, keepdims=True) + NORM_EPS) * g


def _silu(g):
    half = 0.5 * g
    return half + half * jnp.tanh(half)


class _Tiles:
    def __init__(self, batch, seq, tm):
        self.batch, self.seq, self.tm = batch, seq, tm
        self.per_seq = seq // tm
        self.n = batch * self.per_seq

    def tile(self, s, lag):
        return jnp.clip(s - lag, 0, self.n - 1)

    def tok(self, width, lag, col=0):
        return pl.BlockSpec((self.tm, width), lambda s: (self.tile(s, lag), col))

    def res(self, width, dil, lag):
        def index(s):
            j = self.tile(s, lag)
            return (j // self.per_seq, 0, j % self.per_seq, 0)
        return pl.BlockSpec((None, dil, self.tm // dil, width), index)

    def hn_out(self, emit, lag):
        specs, shapes = [], []
        if emit >= 1:
            specs.append(self.tok(D_MODEL, lag))
            shapes.append(jax.ShapeDtypeStruct((self.batch * self.seq, D_MODEL), BF16))
        if emit >= 2:
            for dil in DILATIONS[1:]:
                specs.append(self.res(D_MODEL, dil, lag))
                shapes.append(jax.ShapeDtypeStruct(
                    (self.batch, dil, self.seq // dil, D_MODEL), BF16))
        return specs, shapes


def _const_spec(shape):
    return pl.BlockSpec(shape, lambda s: (0,) * len(shape), pipeline_mode=pl.Buffered(1))


def _perm_inputs(wanted):
    if not wanted:
        return [], []
    mats = []
    for dil in DILATIONS[1:]:
        p = _to_token_order(PERM_TOKENS, dil)
        mats += [p, p.T]
    specs = [_const_spec((PERM_TOKENS, PERM_TOKENS))] * len(mats)
    return [jnp.asarray(m, BF16) for m in mats], specs


def _emit_hn(hn, perm_refs, out_refs):
    if not out_refs:
        return
    hb = hn.astype(BF16)
    out_refs[0][...] = hb
    for k, o_ref in enumerate(out_refs[1:]):
        to_res = perm_refs[2 * k + 1][...]
        dil = o_ref.shape[0]
        rows = PERM_TOKENS // dil
        for t0 in range(0, hb.shape[0], PERM_TOKENS):
            res = jnp.dot(to_res, hb[t0:t0 + PERM_TOKENS],
                          preferred_element_type=F32).astype(BF16)
            for r in range(dil):
                o_ref[r, t0 // dil:t0 // dil + rows] = res[r * rows:(r + 1) * rows]


def _norm_kernel(x_ref, g_ref, *refs):
    perm_refs, out_refs = refs[:4], refs[4:]
    _emit_hn(_rms(x_ref[...], g_ref[...]), perm_refs, out_refs)


def _prenorm(x2, g, tiles):
    perms, perm_specs = _perm_inputs(True)
    out_specs, out_shapes = tiles.hn_out(2, 0)
    return pl.pallas_call(
        _norm_kernel,
        grid=(tiles.n,),
        in_specs=[tiles.tok(D_MODEL, 0), _const_spec((1, D_MODEL))] + perm_specs,
        out_specs=out_specs,
        out_shape=out_shapes,
        compiler_params=_params("parallel"),
        name="prenorm",
    )(x2, g.reshape(1, D_MODEL), *perms)


def _project(hn, w_ref, scale=None, cols=slice(None)):
    w = w_ref[:, cols] if scale is None else w_ref[:, cols] * scale
    return jnp.dot(hn, w.astype(BF16), preferred_element_type=F32)


def _attn_bias_table(g, dil):
    qi = np.arange(BLOCK)[:, None]
    kj = np.arange(2 * BLOCK)[None, :]
    dist = qi + BLOCK - kj
    in_window = (dist >= 0) & (dist <= BLOCK)
    slopes = _alibi_slopes()[g].astype(np.float64) * LOG2E
    bias = -(slopes[:, None, None] * (dil * dist)[None].astype(np.float64))
    general = np.where(in_window[None], bias, -np.inf)
    first = np.where((kj >= BLOCK)[None], general, -np.inf)
    return np.stack([first, general]).astype(np.float32)


def _attend(q, kp, kc, vp, vc, bias):
    k = jnp.concatenate([kp, kc], axis=0)
    v = jnp.concatenate([vp, vc], axis=0)
    s = lax.dot_general(q, k, (((1,), (1,)), ((), ())), preferred_element_type=F32) + bias
    m = jnp.max(s, axis=-1, keepdims=True)
    p = jnp.exp2(s - m)
    l = jnp.sum(p, axis=-1, keepdims=True)
    return jnp.dot(p.astype(BF16), v, preferred_element_type=F32), m, l


def _proj_attn_kernel(hn_ref, wq_ref, wk_ref, wv_ref, bias_ref, o_ref, m_ref, l_ref, slab_ref,
                      halo_ref, *, tiles_per_seq):
    i, hp = pl.program_id(0), pl.program_id(1)
    last = pl.num_programs(0) - 1
    tr = hn_ref.shape[0]
    pair = HEADS_PER_STEP * HEAD_DIM

    @pl.when((i == 0) & (hp == 0))
    def _():
        halo_ref[...] = jnp.zeros_like(halo_ref)

    @pl.when(hp == 0)
    def _():
        m_ref[...] = jnp.zeros_like(m_ref)
        l_ref[...] = jnp.ones_like(l_ref)

    def cols(part, hh):
        return slice(part * pair + hh * HEAD_DIM, part * pair + (hh + 1) * HEAD_DIM)

    def attention():
        first = ((i - 1) % tiles_per_seq) == 0
        lane = lax.broadcasted_iota(jnp.int32, (BLOCK, HEAD_DIM), 1)
        for qb in range(tr // BLOCK):
            rows = slice(qb * BLOCK, (qb + 1) * BLOCK)
            prows = slice((qb - 1) * BLOCK, qb * BLOCK)
            m_t, l_t = m_ref[rows, :], l_ref[rows, :]
            for hh in range(HEADS_PER_STEP):
                if qb == 0:
                    kp, vp = halo_ref[hp, :, cols(0, hh)], halo_ref[hp, :, cols(1, hh)]
                    bias = jnp.where(first, bias_ref[0, hh], bias_ref[1, hh])
                else:
                    kp, vp = slab_ref[hp, prows, cols(1, hh)], slab_ref[hp, prows, cols(2, hh)]
                    bias = bias_ref[1, hh]
                o, m, l = _attend(slab_ref[hp, rows, cols(0, hh)], kp,
                                  slab_ref[hp, rows, cols(1, hh)], vp,
                                  slab_ref[hp, rows, cols(2, hh)], bias)
                o_ref[rows, hh * HEAD_DIM:(hh + 1) * HEAD_DIM] = o.astype(o_ref.dtype)
                head = hp * HEADS_PER_STEP + hh
                m_t = jnp.where(lane == head, m, m_t)
                l_t = jnp.where(lane == head, l, l_t)
            m_ref[rows, :] = m_t
            l_ref[rows, :] = l_t
        halo_ref[hp] = slab_ref[hp, tr - BLOCK:tr, pair:3 * pair]

    def projection():
        hn = hn_ref[...]
        for part, (w_ref, scale) in enumerate(((wq_ref, None), (wk_ref, None), (wv_ref, None))):
            slab_ref[hp, :, part * pair:(part + 1) * pair] = _project(
                hn, w_ref, scale).astype(slab_ref.dtype)

    @pl.when(i == 0)
    def _():
        projection()

    @pl.when((i > 0) & (i < last))
    def _():
        attention()
        projection()

    @pl.when(i == last)
    def _():
        attention()


def _proj_attention(hn, w, g, dil, seq):
    t = hn.shape[0]
    l_sub = seq // dil
    tr = min(ATTN_ROWS, l_sub)
    n_tiles = t // tr
    pair = HEADS_PER_STEP * HEAD_DIM
    n_pairs = N_HEADS // HEADS_PER_STEP
    bias = jnp.asarray(_attn_bias_table(g, dil))

    def tile_done(i, hp):
        return (jnp.maximum(i - 1, 0), 0)

    def w_spec(part):
        first = (part * N_GROUPS + g) * n_pairs
        return pl.BlockSpec((D_MODEL, pair), lambda i, hp: (0, first + hp))

    kern = functools.partial(_proj_attn_kernel, tiles_per_seq=l_sub // tr)
    return pl.pallas_call(
        kern,
        grid=(n_tiles + 1, n_pairs),
        in_specs=[pl.BlockSpec((tr, D_MODEL), lambda i, hp: (jnp.minimum(i, n_tiles - 1), 0)),
                  w_spec(0), w_spec(1), w_spec(2),
                  pl.BlockSpec((2, HEADS_PER_STEP, BLOCK, 2 * BLOCK),
                               lambda i, hp: (0, hp, 0, 0))],
        out_specs=[pl.BlockSpec((tr, pair),
                                lambda i, hp: (jnp.maximum(i - 1, 0), jnp.where(i == 0, 0, hp))),
                   pl.BlockSpec((tr, HEAD_DIM), tile_done),
                   pl.BlockSpec((tr, HEAD_DIM), tile_done)],
        out_shape=[jax.ShapeDtypeStruct((t, ATTN_WIDTH), BF16),
                   jax.ShapeDtypeStruct((t, HEAD_DIM), F32),
                   jax.ShapeDtypeStruct((t, HEAD_DIM), F32)],
        scratch_shapes=[pltpu.VMEM((n_pairs, tr, 3 * pair), BF16),
                        pltpu.VMEM((n_pairs, BLOCK, 2 * pair), BF16)],
        compiler_params=_params("arbitrary", "arbitrary"),
        name=f"proj_attn{g}",
    )(hn, w, w, w, bias)


def _finish(y_ref, x_ref, wout_ref, gpost_ref, gnext_ref, perm_refs, xo_ref, hn_refs):
    out = jnp.dot(y_ref[...], wout_ref[...], preferred_element_type=F32)
    xn = x_ref[...] + _rms(out, gpost_ref[...])
    xo_ref[...] = xn
    if hn_refs:
        _emit_hn(_rms(xn, gnext_ref[...]), perm_refs, hn_refs)


def _attn_out_kernel(o0_ref, o1_ref, o2_ref, m0_ref, m1_ref, m2_ref, l0_ref, l1_ref, l2_ref,
                     hn_ref, x_ref, wgate_ref, wout_ref, gpost_ref, gnext_ref, *refs):
    perm_refs, xo_ref, hn_refs, y_ref = refs[:4], refs[4], refs[5:-1], refs[-1]
    tm = x_ref.shape[0]

    @pl.when(pl.program_id(0) == 0)
    def _():
        y_ref[...] = jnp.zeros_like(y_ref)

    _finish(y_ref, x_ref, wout_ref, gpost_ref, gnext_ref, perm_refs, xo_ref, hn_refs)
    ms = [m0_ref[...], m1_ref[...], m2_ref[...]]
    m = jnp.maximum(jnp.maximum(ms[0], ms[1]), ms[2])
    es = [jnp.exp2(t - m) for t in ms]
    den = es[0] * l0_ref[...] + es[1] * l1_ref[...] + es[2] * l2_ref[...]
    ws = [e / den for e in es]
    hn = hn_ref[...]
    pair = MERGE_HEADS * HEAD_DIM
    for hp in range(N_HEADS // MERGE_HEADS):
        ps = slice(hp * pair, (hp + 1) * pair)
        gate = jnp.dot(hn, wgate_ref[:, ps], preferred_element_type=F32)
        os_ = [o0_ref[:, ps].astype(F32)]
        for k, o_ref in enumerate((o1_ref, o2_ref)):
            res = o_ref[:, :, ps].reshape(tm, pair)
            os_.append(jnp.dot(perm_refs[2 * k][...], res, preferred_element_type=F32))
        for hh in range(MERGE_HEADS):
            h = hp * MERGE_HEADS + hh
            sub = slice(hh * HEAD_DIM, (hh + 1) * HEAD_DIM)
            acc = ws[0][:, h:h + 1] * os_[0][:, sub]
            for g in range(1, N_GROUPS):
                acc = acc + ws[g][:, h:h + 1] * os_[g][:, sub]
            y_ref[:, h * HEAD_DIM:(h + 1) * HEAD_DIM] = (
                acc * _silu(gate[:, sub])).astype(y_ref.dtype)


def _attn_epilogue(os_, stats, hn, x2, wgate, wout, gpost, gnext, tiles, emit):
    tm = tiles.tm
    assert tm == PERM_TOKENS
    perms, perm_specs = _perm_inputs(True)
    hn_specs, hn_shapes = tiles.hn_out(emit, 1)
    return pl.pallas_call(
        _attn_out_kernel,
        grid=(tiles.n + 1,),
        in_specs=[tiles.tok(ATTN_WIDTH, 0), tiles.res(ATTN_WIDTH, DILATIONS[1], 0),
                  tiles.res(ATTN_WIDTH, DILATIONS[2], 0)]
                 + [tiles.tok(HEAD_DIM, 0)] * (2 * N_GROUPS)
                 + [tiles.tok(D_MODEL, 0), tiles.tok(D_MODEL, 1),
                    _const_spec((D_MODEL, ATTN_WIDTH)),
                    _const_spec((ATTN_WIDTH, D_MODEL)), _const_spec((1, D_MODEL)),
                    _const_spec((1, D_MODEL))] + perm_specs,
        out_specs=[tiles.tok(D_MODEL, 1)] + hn_specs,
        out_shape=[jax.ShapeDtypeStruct(x2.shape, F32)] + hn_shapes,
        scratch_shapes=[pltpu.VMEM((tm, ATTN_WIDTH), BF16)],
        compiler_params=_params("arbitrary"),
        name="attn_epilogue",
    )(*os_, *stats, hn, x2, wgate, wout, gpost.reshape(1, -1), gnext.reshape(1, -1), *perms)


def _conv_mix_kernel(hn_ref, wh_ref, wb_ref, wc_ref, wg_ref, cw_ref, y_ref, halo_ref, *,
                     tiles_per_seq):
    i, j = pl.program_id(0), pl.program_id(1)
    tr = hn_ref.shape[0]

    @pl.when((i == 0) & (j == 0))
    def _():
        halo_ref[...] = jnp.zeros_like(halo_ref)

    hn = hn_ref[...]
    first = (i % tiles_per_seq) == 0
    for c0 in range(0, y_ref.shape[1], MXU_COLS):
        cs = slice(c0, c0 + MXU_COLS)
        u = _project(hn, wc_ref, cols=cs) * _project(hn, wh_ref, cols=cs)
        ue = jnp.concatenate([jnp.where(first, 0.0, halo_ref[j, :, cs]), u], axis=0)
        um1 = ue[HALO_ROWS - 1:HALO_ROWS - 1 + tr]
        um2 = ue[HALO_ROWS - 2:HALO_ROWS - 2 + tr]
        halo_ref[j, :, cs] = u[tr - HALO_ROWS:tr]
        cw = cw_ref[:, cs]
        conv = cw[0:1] * um2 + cw[1:2] * um1 + cw[2:3] * u
        y = _project(hn, wb_ref, cols=cs) * conv * _silu(_project(hn, wg_ref, cols=cs))
        y_ref[:, cs] = y.astype(y_ref.dtype)


def _conv_mix(hn, w, conv_w, seq):
    t = hn.shape[0]
    tr = min(PROJ_ROWS, seq)
    n_chunks = D_MODEL // CONV_COLS

    def w_spec(part):
        return pl.BlockSpec((D_MODEL, CONV_COLS), lambda i, j: (0, part * n_chunks + j))

    kern = functools.partial(_conv_mix_kernel, tiles_per_seq=seq // tr)
    return pl.pallas_call(
        kern,
        grid=(t // tr, n_chunks),
        in_specs=[pl.BlockSpec((tr, D_MODEL), lambda i, j: (i, 0)),
                  w_spec(0), w_spec(1), w_spec(2), w_spec(3),
                  pl.BlockSpec((CONV_K, CONV_COLS), lambda i, j: (0, j))],
        out_specs=pl.BlockSpec((tr, CONV_COLS), lambda i, j: (i, j)),
        out_shape=jax.ShapeDtypeStruct((t, D_MODEL), BF16),
        scratch_shapes=[pltpu.VMEM((n_chunks, HALO_ROWS, CONV_COLS), F32)],
        compiler_params=_params("arbitrary", "arbitrary"),
        name="conv_mix",
    )(hn, w, w, w, w, conv_w)


def _out_kernel(y_ref, x_ref, wout_ref, gpost_ref, gnext_ref, *refs, n_perm):
    perm_refs, xo_ref, hn_refs = refs[:n_perm], refs[n_perm], refs[n_perm + 1:]
    _finish(y_ref, x_ref, wout_ref, gpost_ref, gnext_ref, perm_refs, xo_ref, hn_refs)


def _out_epilogue(y, x2, wout, gpost, gnext, tiles, emit):
    perms, perm_specs = _perm_inputs(emit >= 2)
    hn_specs, hn_shapes = tiles.hn_out(emit, 0)
    kern = functools.partial(_out_kernel, n_perm=len(perms))
    return pl.pallas_call(
        kern,
        grid=(tiles.n,),
        in_specs=[tiles.tok(D_MODEL, 0), tiles.tok(D_MODEL, 0),
                  _const_spec((D_MODEL, D_MODEL)), _const_spec((1, D_MODEL)),
                  _const_spec((1, D_MODEL))] + perm_specs,
        out_specs=[tiles.tok(D_MODEL, 0)] + hn_specs,
        out_shape=[jax.ShapeDtypeStruct(x2.shape, F32)] + hn_shapes,
        compiler_params=_params("parallel"),
        name="out_epilogue",
    )(y, x2, wout, gpost.reshape(1, -1), gnext.reshape(1, -1), *perms)


def _attention_layer(x2, hns, w_in, w_out, gpost, gnext, tiles, emit):
    t = x2.shape[0]
    batch, seq = tiles.batch, tiles.seq
    os_, ms, ls = [], [], []
    col_scale = jnp.where(jnp.arange(w_in.shape[1]) < N_GROUPS * ATTN_WIDTH, Q_SCALE, 1.0)
    w_bf = (w_in * col_scale.astype(F32)[None, :]).astype(BF16)
    for g, dil in enumerate(DILATIONS):
        o, m, l = _proj_attention(hns[g].reshape(t, D_MODEL), w_bf, g, dil, seq)
        os_.append(o if dil == 1 else o.reshape(batch, dil, seq // dil, ATTN_WIDTH))

        def to_token_order(a, dil=dil):
            a = a.reshape(batch, dil, seq // dil, HEAD_DIM)
            return jnp.swapaxes(a, 1, 2).reshape(t, HEAD_DIM)

        ms.append(to_token_order(m))
        ls.append(to_token_order(l))
    w_gate = w_in[:, 3 * N_GROUPS * ATTN_WIDTH:].astype(BF16)
    return _attn_epilogue(os_, ms + ls, hns[0], x2, w_gate, w_out.astype(BF16), gpost, gnext,
                          tiles, emit)


def _conv_layer(x2, hn, w_in, conv_w, w_out, gpost, gnext, tiles, emit):
    y = _conv_mix(hn, w_in, conv_w, tiles.seq)
    out_tiles = _Tiles(tiles.batch, tiles.seq, OUT_TOKENS)
    return _out_epilogue(y, x2, w_out.astype(BF16), gpost, gnext, out_tiles, emit)


def kernel(x, l0_norm_pre, l0_w_in, l0_w_out, l0_norm_post, l1_norm_pre, l1_w_in, l1_conv_w, l1_w_out, l1_norm_post, l2_norm_pre, l2_w_in, l2_w_out, l2_norm_post, l3_norm_pre, l3_w_in, l3_conv_w, l3_w_out, l3_norm_post):
    batch, seq, d = x.shape
    assert d == D_MODEL and seq % (DILATIONS[-1] * BLOCK) == 0
    tiles = _Tiles(batch, seq, EPILOGUE_TOKENS)
    x2 = x.reshape(batch * seq, D_MODEL)
    hns = _prenorm(x2, l0_norm_pre, _Tiles(batch, seq, OUT_TOKENS))
    x2, hn = _attention_layer(x2, hns, l0_w_in, l0_w_out, l0_norm_post, l1_norm_pre, tiles, 1)
    x2, *hns = _conv_layer(x2, hn, l1_w_in, l1_conv_w, l1_w_out, l1_norm_post, l2_norm_pre, tiles, 2)
    x2, hn = _attention_layer(x2, hns, l2_w_in, l2_w_out, l2_norm_post, l3_norm_pre, tiles, 1)
    (x2,) = _conv_layer(x2, hn, l3_w_in, l3_conv_w, l3_w_out, l3_norm_post, l3_norm_pre, tiles, 0)
    return x2.reshape(batch, seq, D_MODEL)
```

```python
import functools

import jax
import jax.numpy as jnp
import numpy as np
from jax import lax
from jax.experimental import pallas as pl
from jax.experimental.pallas import tpu as pltpu

D_MODEL = 2048
HEAD_DIM = 128
N_HEADS = 16
ATTN_WIDTH = N_HEADS * HEAD_DIM
DILATIONS = (1, 4, 16)
N_GROUPS = len(DILATIONS)
BLOCK = 128
CONV_K = 3
NORM_EPS = 1e-6
HALO_ROWS = 8
V7X_VMEM_LIMIT_BYTES = 56 * 1024 * 1024
MXU_COLS = 256
PERM_TOKENS = 256
EPILOGUE_TOKENS = PERM_TOKENS
OUT_TOKENS = 2 * PERM_TOKENS
PROJ_ROWS = 1024
CONV_COLS = 512
ATTN_ROWS = 1024
HEADS_PER_STEP = 4
MERGE_HEADS = 2
LOG2E = 1.4426950408889634
Q_SCALE = HEAD_DIM ** -0.5 * LOG2E

BF16 = jnp.bfloat16
F32 = jnp.float32


def _alibi_slopes():
    n = N_GROUPS * N_HEADS
    s = 2.0 ** (-8.0 * np.arange(1, n + 1) / n)
    return s.reshape(N_GROUPS, N_HEADS).astype(np.float32)


def _to_token_order(tm, dil):
    p = np.zeros((tm, tm), np.float32)
    u, r = np.meshgrid(np.arange(tm // dil), np.arange(dil), indexing="ij")
    p[(u * dil + r).ravel(), (r * (tm // dil) + u).ravel()] = 1.0
    return p


def _params(*semantics):
    return pltpu.CompilerParams(dimension_semantics=semantics,
                                vmem_limit_bytes=V7X_VMEM_LIMIT_BYTES)


def _rms(v, g):
    return v * lax.rsqrt(jnp.mean(v * v, axis=-1, keepdims=True) + NORM_EPS) * g


def _silu(g):
    half = 0.5 * g
    return half + half * jnp.tanh(half)


class _Tiles:
    def __init__(self, batch, seq, tm):
        self.batch, self.seq, self.tm = batch, seq, tm
        self.per_seq = seq // tm
        self.n = batch * self.per_seq

    def tile(self, s, lag):
        return jnp.clip(s - lag, 0, self.n - 1)

    def tok(self, width, lag, col=0):
        return pl.BlockSpec((self.tm, width), lambda s: (self.tile(s, lag), col))

    def res(self, width, dil, lag):
        def index(s):
            j = self.tile(s, lag)
            return (j // self.per_seq, 0, j % self.per_seq, 0)
        return pl.BlockSpec((None, dil, self.tm // dil, width), index)

    def hn_out(self, emit, lag):
        specs, shapes = [], []
        if emit >= 1:
            specs.append(self.tok(D_MODEL, lag))
            shapes.append(jax.ShapeDtypeStruct((self.batch * self.seq, D_MODEL), BF16))
        if emit >= 2:
            for dil in DILATIONS[1:]:
                specs.append(self.res(D_MODEL, dil, lag))
                shapes.append(jax.ShapeDtypeStruct(
                    (self.batch, dil, self.seq // dil, D_MODEL), BF16))
        return specs, shapes


def _const_spec(shape):
    return pl.BlockSpec(shape, lambda s: (0,) * len(shape), pipeline_mode=pl.Buffered(1))


def _perm_inputs(wanted):
    if not wanted:
        return [], []
    mats = []
    for dil in DILATIONS[1:]:
        p = _to_token_order(PERM_TOKENS, dil)
        mats += [p, p.T]
    specs = [_const_spec((PERM_TOKENS, PERM_TOKENS))] * len(mats)
    return [jnp.asarray(m, BF16) for m in mats], specs


def _emit_hn(hn, perm_refs, out_refs):
    if not out_refs:
        return
    hb = hn.astype(BF16)
    out_refs[0][...] = hb
    for k, o_ref in enumerate(out_refs[1:]):
        to_res = perm_refs[2 * k + 1][...]
        dil = o_ref.shape[0]
        rows = PERM_TOKENS // dil
        for t0 in range(0, hb.shape[0], PERM_TOKENS):
            res = jnp.dot(to_res, hb[t0:t0 + PERM_TOKENS],
                          preferred_element_type=F32).astype(BF16)
            for r in range(dil):
                o_ref[r, t0 // dil:t0 // dil + rows] = res[r * rows:(r + 1) * rows]


def _norm_kernel(x_ref, g_ref, *refs):
    perm_refs, out_refs = refs[:4], refs[4:]
    _emit_hn(_rms(x_ref[...], g_ref[...]), perm_refs, out_refs)


def _prenorm(x2, g, tiles):
    perms, perm_specs = _perm_inputs(True)
    out_specs, out_shapes = tiles.hn_out(2, 0)
    return pl.pallas_call(
        _norm_kernel,
        grid=(tiles.n,),
        in_specs=[tiles.tok(D_MODEL, 0), _const_spec((1, D_MODEL))] + perm_specs,
        out_specs=out_specs,
        out_shape=out_shapes,
        compiler_params=_params("parallel"),
        name="prenorm",
    )(x2, g.reshape(1, D_MODEL), *perms)


def _project(hn, w_ref, scale=None, cols=slice(None)):
    w = w_ref[:, cols] if scale is None else w_ref[:, cols] * scale
    return jnp.dot(hn, w.astype(BF16), preferred_element_type=F32)


def _attn_bias_table(g, dil):
    qi = np.arange(BLOCK)[:, None]
    kj = np.arange(2 * BLOCK)[None, :]
    dist = qi + BLOCK - kj
    in_window = (dist >= 0) & (dist <= BLOCK)
    slopes = _alibi_slopes()[g].astype(np.float64) * LOG2E
    bias = -(slopes[:, None, None] * (dil * dist)[None].astype(np.float64))
    general = np.where(in_window[None], bias, -np.inf)
    first = np.where((kj >= BLOCK)[None], general, -np.inf)
    return np.stack([first, general]).astype(np.float32)


def _attend(q, kp, kc, vp, vc, bias):
    k = jnp.concatenate([kp, kc], axis=0)
    v = jnp.concatenate([vp, vc], axis=0)
    s = lax.dot_general(q, k, (((1,), (1,)), ((), ())), preferred_element_type=F32) + bias
    m = jnp.max(s, axis=-1, keepdims=True)
    p = jnp.exp2(s - m)
    l = jnp.sum(p, axis=-1, keepdims=True)
    return jnp.dot(p.astype(BF16), v, preferred_element_type=F32), m, l


def _proj_attn_kernel(hn_ref, wq_ref, wk_ref, wv_ref, bias_ref, o_ref, ml_ref, slab_ref,
                      halo_ref, *, tiles_per_seq):
    i, hp = pl.program_id(0), pl.program_id(1)
    last = pl.num_programs(0) - 1
    tr = hn_ref.shape[0]
    pair = HEADS_PER_STEP * HEAD_DIM

    @pl.when((i == 0) & (hp == 0))
    def _():
        halo_ref[...] = jnp.zeros_like(halo_ref)

    @pl.when(hp == 0)
    def _():
        ml_ref[:, :HEAD_DIM] = jnp.zeros((tr, HEAD_DIM), F32)
        ml_ref[:, HEAD_DIM:] = jnp.ones((tr, HEAD_DIM), F32)

    def cols(part, hh):
        return slice(part * pair + hh * HEAD_DIM, part * pair + (hh + 1) * HEAD_DIM)

    def attention():
        first = ((i - 1) % tiles_per_seq) == 0
        lane = lax.broadcasted_iota(jnp.int32, (BLOCK, HEAD_DIM), 1)
        for qb in range(tr // BLOCK):
            rows = slice(qb * BLOCK, (qb + 1) * BLOCK)
            prows = slice((qb - 1) * BLOCK, qb * BLOCK)
            m_t, l_t = ml_ref[rows, :HEAD_DIM], ml_ref[rows, HEAD_DIM:]
            for hh in range(HEADS_PER_STEP):
                if qb == 0:
                    kp, vp = halo_ref[hp, :, cols(0, hh)], halo_ref[hp, :, cols(1, hh)]
                    bias = jnp.where(first, bias_ref[0, hh], bias_ref[1, hh])
                else:
                    kp, vp = slab_ref[hp, prows, cols(1, hh)], slab_ref[hp, prows, cols(2, hh)]
                    bias = bias_ref[1, hh]
                o, m, l = _attend(slab_ref[hp, rows, cols(0, hh)], kp,
                                  slab_ref[hp, rows, cols(1, hh)], vp,
                                  slab_ref[hp, rows, cols(2, hh)], bias)
                o_ref[rows, hh * HEAD_DIM:(hh + 1) * HEAD_DIM] = o.astype(o_ref.dtype)
                head = hp * HEADS_PER_STEP + hh
                m_t = jnp.where(lane == head, m, m_t)
                l_t = jnp.where(lane == head, l, l_t)
            ml_ref[rows, :HEAD_DIM] = m_t
            ml_ref[rows, HEAD_DIM:] = l_t
        halo_ref[hp] = slab_ref[hp, tr - BLOCK:tr, pair:3 * pair]

    def projection():
        hn = hn_ref[...]
        for part, (w_ref, scale) in enumerate(((wq_ref, Q_SCALE), (wk_ref, None), (wv_ref, None))):
            slab_ref[hp, :, part * pair:(part + 1) * pair] = _project(
                hn, w_ref, scale).astype(slab_ref.dtype)

    @pl.when(i == 0)
    def _():
        projection()

    @pl.when((i > 0) & (i < last))
    def _():
        attention()
        projection()

    @pl.when(i == last)
    def _():
        attention()


def _proj_attention(hn, w, g, dil, seq):
    t = hn.shape[0]
    l_sub = seq // dil
    tr = min(ATTN_ROWS, l_sub)
    n_tiles = t // tr
    pair = HEADS_PER_STEP * HEAD_DIM
    n_pairs = N_HEADS // HEADS_PER_STEP
    bias = jnp.asarray(_attn_bias_table(g, dil))

    def tile_done(i, hp):
        return (jnp.maximum(i - 1, 0), 0)

    def w_spec(part):
        first = (part * N_GROUPS + g) * n_pairs
        return pl.BlockSpec((D_MODEL, pair), lambda i, hp: (0, first + hp))

    kern = functools.partial(_proj_attn_kernel, tiles_per_seq=l_sub // tr)
    return pl.pallas_call(
        kern,
        grid=(n_tiles + 1, n_pairs),
        in_specs=[pl.BlockSpec((tr, D_MODEL), lambda i, hp: (jnp.minimum(i, n_tiles - 1), 0)),
                  w_spec(0), w_spec(1), w_spec(2),
                  pl.BlockSpec((2, HEADS_PER_STEP, BLOCK, 2 * BLOCK),
                               lambda i, hp: (0, hp, 0, 0))],
        out_specs=[pl.BlockSpec((tr, pair),
                                lambda i, hp: (jnp.maximum(i - 1, 0), jnp.where(i == 0, 0, hp))),
                   pl.BlockSpec((tr, 2 * HEAD_DIM), tile_done)],
        out_shape=[jax.ShapeDtypeStruct((t, ATTN_WIDTH), BF16),
                   jax.ShapeDtypeStruct((t, 2 * HEAD_DIM), F32)],
        scratch_shapes=[pltpu.VMEM((n_pairs, tr, 3 * pair), BF16),
                        pltpu.VMEM((n_pairs, BLOCK, 2 * pair), BF16)],
        compiler_params=_params("arbitrary", "arbitrary"),
        name=f"proj_attn{g}",
    )(hn, w, w, w, bias)


def _finish(y_ref, x_ref, wout_ref, gpost_ref, gnext_ref, perm_refs, xo_ref, hn_refs):
    out = jnp.dot(y_ref[...], wout_ref[...], preferred_element_type=F32)
    xn = x_ref[...] + _rms(out, gpost_ref[...])
    xo_ref[...] = xn
    if hn_refs:
        _emit_hn(_rms(xn, gnext_ref[...]), perm_refs, hn_refs)


def _attn_out_kernel(o0_ref, o1_ref, o2_ref, ml0_ref, ml1_ref, ml2_ref,
                     hn_ref, x_ref, wgate_ref, wout_ref, gpost_ref, gnext_ref, *refs):
    perm_refs, xo_ref, hn_refs, y_ref = refs[:4], refs[4], refs[5:-1], refs[-1]
    tm = x_ref.shape[0]

    @pl.when(pl.program_id(0) == 0)
    def _():
        y_ref[...] = jnp.zeros_like(y_ref)

    _finish(y_ref, x_ref, wout_ref, gpost_ref, gnext_ref, perm_refs, xo_ref, hn_refs)
    ms = [r[:, :HEAD_DIM] for r in (ml0_ref, ml1_ref, ml2_ref)]
    ls = [r[:, HEAD_DIM:] for r in (ml0_ref, ml1_ref, ml2_ref)]
    m = jnp.maximum(jnp.maximum(ms[0], ms[1]), ms[2])
    es = [jnp.exp2(t - m) for t in ms]
    den = es[0] * ls[0] + es[1] * ls[1] + es[2] * ls[2]
    ws = [e / den for e in es]
    hn = hn_ref[...]
    pair = MERGE_HEADS * HEAD_DIM
    for hp in range(N_HEADS // MERGE_HEADS):
        ps = slice(hp * pair, (hp + 1) * pair)
        gate = jnp.dot(hn, wgate_ref[:, ps], preferred_element_type=F32)
        os_ = [o0_ref[:, ps].astype(F32)]
        for k, o_ref in enumerate((o1_ref, o2_ref)):
            res = o_ref[:, :, ps].reshape(tm, pair)
            os_.append(jnp.dot(perm_refs[2 * k][...], res, preferred_element_type=F32))
        for hh in range(MERGE_HEADS):
            h = hp * MERGE_HEADS + hh
            sub = slice(hh * HEAD_DIM, (hh + 1) * HEAD_DIM)
            acc = ws[0][:, h:h + 1] * os_[0][:, sub]
            for g in range(1, N_GROUPS):
                acc = acc + ws[g][:, h:h + 1] * os_[g][:, sub]
            y_ref[:, h * HEAD_DIM:(h + 1) * HEAD_DIM] = (
                acc * _silu(gate[:, sub])).astype(y_ref.dtype)


def _attn_epilogue(os_, stats, hn, x2, wgate, wout, gpost, gnext, tiles, emit):
    tm = tiles.tm
    assert tm == PERM_TOKENS
    perms, perm_specs = _perm_inputs(True)
    hn_specs, hn_shapes = tiles.hn_out(emit, 1)
    return pl.pallas_call(
        _attn_out_kernel,
        grid=(tiles.n + 1,),
        in_specs=[tiles.tok(ATTN_WIDTH, 0), tiles.res(ATTN_WIDTH, DILATIONS[1], 0),
                  tiles.res(ATTN_WIDTH, DILATIONS[2], 0)]
                 + [tiles.tok(2 * HEAD_DIM, 0)] * N_GROUPS
                 + [tiles.tok(D_MODEL, 0), tiles.tok(D_MODEL, 1),
                    _const_spec((D_MODEL, ATTN_WIDTH)),
                    _const_spec((ATTN_WIDTH, D_MODEL)), _const_spec((1, D_MODEL)),
                    _const_spec((1, D_MODEL))] + perm_specs,
        out_specs=[tiles.tok(D_MODEL, 1)] + hn_specs,
        out_shape=[jax.ShapeDtypeStruct(x2.shape, F32)] + hn_shapes,
        scratch_shapes=[pltpu.VMEM((tm, ATTN_WIDTH), BF16)],
        compiler_params=_params("arbitrary"),
        name="attn_epilogue",
    )(*os_, *stats, hn, x2, wgate, wout, gpost.reshape(1, -1), gnext.reshape(1, -1), *perms)


def _conv_mix_kernel(hn_ref, wh_ref, wb_ref, wc_ref, wg_ref, cw_ref, y_ref, halo_ref, *,
                     tiles_per_seq):
    i, j = pl.program_id(0), pl.program_id(1)
    tr = hn_ref.shape[0]

    @pl.when((i == 0) & (j == 0))
    def _():
        halo_ref[...] = jnp.zeros_like(halo_ref)

    hn = hn_ref[...]
    first = (i % tiles_per_seq) == 0
    for c0 in range(0, y_ref.shape[1], MXU_COLS):
        cs = slice(c0, c0 + MXU_COLS)
        u = _project(hn, wc_ref, cols=cs) * _project(hn, wh_ref, cols=cs)
        ue = jnp.concatenate([jnp.where(first, 0.0, halo_ref[j, :, cs]), u], axis=0)
        um1 = ue[HALO_ROWS - 1:HALO_ROWS - 1 + tr]
        um2 = ue[HALO_ROWS - 2:HALO_ROWS - 2 + tr]
        halo_ref[j, :, cs] = u[tr - HALO_ROWS:tr]
        cw = cw_ref[:, cs]
        conv = cw[0:1] * um2 + cw[1:2] * um1 + cw[2:3] * u
        y = _project(hn, wb_ref, cols=cs) * conv * _silu(_project(hn, wg_ref, cols=cs))
        y_ref[:, cs] = y.astype(y_ref.dtype)


def _conv_mix(hn, w, conv_w, seq):
    t = hn.shape[0]
    tr = min(PROJ_ROWS, seq)
    n_chunks = D_MODEL // CONV_COLS

    def w_spec(part):
        return pl.BlockSpec((D_MODEL, CONV_COLS), lambda i, j: (0, part * n_chunks + j))

    kern = functools.partial(_conv_mix_kernel, tiles_per_seq=seq // tr)
    return pl.pallas_call(
        kern,
        grid=(t // tr, n_chunks),
        in_specs=[pl.BlockSpec((tr, D_MODEL), lambda i, j: (i, 0)),
                  w_spec(0), w_spec(1), w_spec(2), w_spec(3),
                  pl.BlockSpec((CONV_K, CONV_COLS), lambda i, j: (0, j))],
        out_specs=pl.BlockSpec((tr, CONV_COLS), lambda i, j: (i, j)),
        out_shape=jax.ShapeDtypeStruct((t, D_MODEL), BF16),
        scratch_shapes=[pltpu.VMEM((n_chunks, HALO_ROWS, CONV_COLS), F32)],
        compiler_params=_params("arbitrary", "arbitrary"),
        name="conv_mix",
    )(hn, w, w, w, w, conv_w)


def _out_kernel(y_ref, x_ref, wout_ref, gpost_ref, gnext_ref, *refs, n_perm):
    perm_refs, xo_ref, hn_refs = refs[:n_perm], refs[n_perm], refs[n_perm + 1:]
    _finish(y_ref, x_ref, wout_ref, gpost_ref, gnext_ref, perm_refs, xo_ref, hn_refs)


def _out_epilogue(y, x2, wout, gpost, gnext, tiles, emit):
    perms, perm_specs = _perm_inputs(emit >= 2)
    hn_specs, hn_shapes = tiles.hn_out(emit, 0)
    kern = functools.partial(_out_kernel, n_perm=len(perms))
    return pl.pallas_call(
        kern,
        grid=(tiles.n,),
        in_specs=[tiles.tok(D_MODEL, 0), tiles.tok(D_MODEL, 0),
                  _const_spec((D_MODEL, D_MODEL)), _const_spec((1, D_MODEL)),
                  _const_spec((1, D_MODEL))] + perm_specs,
        out_specs=[tiles.tok(D_MODEL, 0)] + hn_specs,
        out_shape=[jax.ShapeDtypeStruct(x2.shape, F32)] + hn_shapes,
        compiler_params=_params("parallel"),
        name="out_epilogue",
    )(y, x2, wout, gpost.reshape(1, -1), gnext.reshape(1, -1), *perms)


def _attention_layer(x2, hns, w_in, w_out, gpost, gnext, tiles, emit):
    t = x2.shape[0]
    batch, seq = tiles.batch, tiles.seq
    os_, stats = [], []
    for g, dil in enumerate(DILATIONS):
        o, ml = _proj_attention(hns[g].reshape(t, D_MODEL), w_in, g, dil, seq)
        os_.append(o if dil == 1 else o.reshape(batch, dil, seq // dil, ATTN_WIDTH))
        ml = ml.reshape(batch, dil, seq // dil, 2 * HEAD_DIM)
        stats.append(jnp.swapaxes(ml, 1, 2).reshape(t, 2 * HEAD_DIM))
    w_gate = w_in[:, 3 * N_GROUPS * ATTN_WIDTH:].astype(BF16)
    return _attn_epilogue(os_, stats, hns[0], x2, w_gate, w_out.astype(BF16), gpost, gnext,
                          tiles, emit)


def _conv_layer(x2, hn, w_in, conv_w, w_out, gpost, gnext, tiles, emit):
    y = _conv_mix(hn, w_in, conv_w, tiles.seq)
    out_tiles = _Tiles(tiles.batch, tiles.seq, OUT_TOKENS)
    return _out_epilogue(y, x2, w_out.astype(BF16), gpost, gnext, out_tiles, emit)


def kernel(x, l0_norm_pre, l0_w_in, l0_w_out, l0_norm_post, l1_norm_pre, l1_w_in, l1_conv_w, l1_w_out, l1_norm_post, l2_norm_pre, l2_w_in, l2_w_out, l2_norm_post, l3_norm_pre, l3_w_in, l3_conv_w, l3_w_out, l3_norm_post):
    batch, seq, d = x.shape
    assert d == D_MODEL and seq % (DILATIONS[-1] * BLOCK) == 0
    tiles = _Tiles(batch, seq, EPILOGUE_TOKENS)
    x2 = x.reshape(batch * seq, D_MODEL)
    hns = _prenorm(x2, l0_norm_pre, _Tiles(batch, seq, OUT_TOKENS))
    x2, hn = _attention_layer(x2, hns, l0_w_in, l0_w_out, l0_norm_post, l1_norm_pre, tiles, 1)
    x2, *hns = _conv_layer(x2, hn, l1_w_in, l1_conv_w, l1_w_out, l1_norm_post, l2_norm_pre, tiles, 2)
    x2, hn = _attention_layer(x2, hns, l2_w_in, l2_w_out, l2_norm_post, l3_norm_pre, tiles, 1)
    (x2,) = _conv_layer(x2, hn, l3_w_in, l3_conv_w, l3_w_out, l3_norm_post, l3_norm_pre, tiles, 0)
    return x2.reshape(batch, seq, D_MODEL)
```

```python
import functools

import jax
import jax.numpy as jnp
import numpy as np
from jax import lax
from jax.experimental import pallas as pl
from jax.experimental.pallas import tpu as pltpu

D_MODEL = 2048
HEAD_DIM = 128
N_HEADS = 16
ATTN_WIDTH = N_HEADS * HEAD_DIM
DILATIONS = (1, 4, 16)
N_GROUPS = len(DILATIONS)
BLOCK = 128
CONV_K = 3
NORM_EPS = 1e-6
HALO_ROWS = 8
V7X_VMEM_LIMIT_BYTES = 56 * 1024 * 1024
MXU_COLS = 256
PERM_TOKENS = 256
EPILOGUE_TOKENS = PERM_TOKENS
OUT_TOKENS = 2 * PERM_TOKENS
PROJ_ROWS = 1024
CONV_COLS = 512
ATTN_ROWS = 1024
HEADS_PER_STEP = 4
MERGE_HEADS = 2
LOG2E = 1.4426950408889634
Q_SCALE = HEAD_DIM ** -0.5 * LOG2E

BF16 = jnp.bfloat16
F32 = jnp.float32


def _alibi_slopes():
    n = N_GROUPS * N_HEADS
    s = 2.0 ** (-8.0 * np.arange(1, n + 1) / n)
    return s.reshape(N_GROUPS, N_HEADS).astype(np.float32)


def _to_token_order(tm, dil):
    p = np.zeros((tm, tm), np.float32)
    u, r = np.meshgrid(np.arange(tm // dil), np.arange(dil), indexing="ij")
    p[(u * dil + r).ravel(), (r * (tm // dil) + u).ravel()] = 1.0
    return p


def _params(*semantics):
    return pltpu.CompilerParams(dimension_semantics=semantics,
                                vmem_limit_bytes=V7X_VMEM_LIMIT_BYTES)


def _rms(v, g):
    return v * lax.rsqrt(jnp.mean(v * v, axis=-1, keepdims=True) + NORM_EPS) * g


def _silu(g):
    half = 0.5 * g
    return half + half * jnp.tanh(half)


class _Tiles:
    def __init__(self, batch, seq, tm):
        self.batch, self.seq, self.tm = batch, seq, tm
        self.per_seq = seq // tm
        self.n = batch * self.per_seq

    def tile(self, s, lag):
        return jnp.clip(s - lag, 0, self.n - 1)

    def tok(self, width, lag, col=0):
        return pl.BlockSpec((self.tm, width), lambda s: (self.tile(s, lag), col))

    def res(self, width, dil, lag):
        def index(s):
            j = self.tile(s, lag)
            return (j // self.per_seq, 0, j % self.per_seq, 0)
        return pl.BlockSpec((None, dil, self.tm // dil, width), index)

    def hn_out(self, emit, lag):
        specs, shapes = [], []
        if emit >= 1:
            specs.append(self.tok(D_MODEL, lag))
            shapes.append(jax.ShapeDtypeStruct((self.batch * self.seq, D_MODEL), BF16))
        if emit >= 2:
            for dil in DILATIONS[1:]:
                specs.append(self.res(D_MODEL, dil, lag))
                shapes.append(jax.ShapeDtypeStruct(
                    (self.batch, dil, self.seq // dil, D_MODEL), BF16))
        return specs, shapes


def _const_spec(shape):
    return pl.BlockSpec(shape, lambda s: (0,) * len(shape), pipeline_mode=pl.Buffered(1))


def _perm_inputs(wanted):
    if not wanted:
        return [], []
    mats = []
    for dil in DILATIONS[1:]:
        p = _to_token_order(PERM_TOKENS, dil)
        mats += [p, p.T]
    specs = [_const_spec((PERM_TOKENS, PERM_TOKENS))] * len(mats)
    return [jnp.asarray(m, BF16) for m in mats], specs


def _emit_hn(hn, perm_refs, out_refs):
    if not out_refs:
        return
    hb = hn.astype(BF16)
    out_refs[0][...] = hb
    for k, o_ref in enumerate(out_refs[1:]):
        to_res = perm_refs[2 * k + 1][...]
        dil = o_ref.shape[0]
        rows = PERM_TOKENS // dil
        for t0 in range(0, hb.shape[0], PERM_TOKENS):
            res = jnp.dot(to_res, hb[t0:t0 + PERM_TOKENS],
                          preferred_element_type=F32).astype(BF16)
            for r in range(dil):
                o_ref[r, t0 // dil:t0 // dil + rows] = res[r * rows:(r + 1) * rows]


def _norm_kernel(x_ref, g_ref, *refs):
    perm_refs, out_refs = refs[:4], refs[4:]
    _emit_hn(_rms(x_ref[...], g_ref[...]), perm_refs, out_refs)


def _prenorm(x2, g, tiles):
    perms, perm_specs = _perm_inputs(True)
    out_specs, out_shapes = tiles.hn_out(2, 0)
    return pl.pallas_call(
        _norm_kernel,
        grid=(tiles.n,),
        in_specs=[tiles.tok(D_MODEL, 0), _const_spec((1, D_MODEL))] + perm_specs,
        out_specs=out_specs,
        out_shape=out_shapes,
        compiler_params=_params("parallel"),
        name="prenorm",
    )(x2, g.reshape(1, D_MODEL), *perms)


def _project(hn, w_ref, scale=None, cols=slice(None)):
    w = w_ref[:, cols] if scale is None else w_ref[:, cols] * scale
    return jnp.dot(hn, w.astype(BF16), preferred_element_type=F32)


def _attn_bias_table(g, dil):
    qi = np.arange(BLOCK)[:, None]
    kj = np.arange(2 * BLOCK)[None, :]
    dist = qi + BLOCK - kj
    in_window = (dist >= 0) & (dist <= BLOCK)
    slopes = _alibi_slopes()[g].astype(np.float64) * LOG2E
    bias = -(slopes[:, None, None] * (dil * dist)[None].astype(np.float64))
    general = np.where(in_window[None], bias, -np.inf)
    first = np.where((kj >= BLOCK)[None], general, -np.inf)
    return np.stack([first, general]).astype(np.float32)


def _attend(q, kp, kc, vp, vc, bias):
    k = jnp.concatenate([kp, kc], axis=0)
    v = jnp.concatenate([vp, vc], axis=0)
    s = lax.dot_general(q, k, (((1,), (1,)), ((), ())), preferred_element_type=F32) + bias
    m = jnp.max(s, axis=-1, keepdims=True)
    p = jnp.exp2(s - m)
    l = jnp.sum(p, axis=-1, keepdims=True)
    return jnp.dot(p.astype(BF16), v, preferred_element_type=F32), m, l


def _proj_attn_kernel(hn_ref, wq_ref, wk_ref, wv_ref, bias_ref, o_ref, m_ref, l_ref, slab_ref,
                      halo_ref, *, tiles_per_seq):
    i, hp = pl.program_id(0), pl.program_id(1)
    last = pl.num_programs(0) - 1
    tr = hn_ref.shape[0]
    pair = HEADS_PER_STEP * HEAD_DIM

    @pl.when((i == 0) & (hp == 0))
    def _():
        halo_ref[...] = jnp.zeros_like(halo_ref)

    @pl.when(hp == 0)
    def _():
        m_ref[...] = jnp.zeros_like(m_ref)
        l_ref[...] = jnp.ones_like(l_ref)

    def cols(part, hh):
        return slice(part * pair + hh * HEAD_DIM, part * pair + (hh + 1) * HEAD_DIM)

    def attention():
        first = ((i - 1) % tiles_per_seq) == 0
        lane = lax.broadcasted_iota(jnp.int32, (BLOCK, HEAD_DIM), 1)
        for qb in range(tr // BLOCK):
            rows = slice(qb * BLOCK, (qb + 1) * BLOCK)
            prows = slice((qb - 1) * BLOCK, qb * BLOCK)
            m_t, l_t = m_ref[rows, :], l_ref[rows, :]
            for hh in range(HEADS_PER_STEP):
                if qb == 0:
                    kp, vp = halo_ref[hp, :, cols(0, hh)], halo_ref[hp, :, cols(1, hh)]
                    bias = jnp.where(first, bias_ref[0, hh], bias_ref[1, hh])
                else:
                    kp, vp = slab_ref[hp, prows, cols(1, hh)], slab_ref[hp, prows, cols(2, hh)]
                    bias = bias_ref[1, hh]
                o, m, l = _attend(slab_ref[hp, rows, cols(0, hh)], kp,
                                  slab_ref[hp, rows, cols(1, hh)], vp,
                                  slab_ref[hp, rows, cols(2, hh)], bias)
                o_ref[rows, hh * HEAD_DIM:(hh + 1) * HEAD_DIM] = o.astype(o_ref.dtype)
                head = hp * HEADS_PER_STEP + hh
                m_t = jnp.where(lane == head, m, m_t)
                l_t = jnp.where(lane == head, l, l_t)
            m_ref[rows, :] = m_t
            l_ref[rows, :] = l_t
        halo_ref[hp] = slab_ref[hp, tr - BLOCK:tr, pair:3 * pair]

    def projection():
        hn = hn_ref[...]
        for part, (w_ref, scale) in enumerate(((wq_ref, Q_SCALE), (wk_ref, None), (wv_ref, None))):
            slab_ref[hp, :, part * pair:(part + 1) * pair] = _project(
                hn, w_ref, scale).astype(slab_ref.dtype)

    @pl.when(i == 0)
    def _():
        projection()

    @pl.when((i > 0) & (i < last))
    def _():
        attention()
        projection()

    @pl.when(i == last)
    def _():
        attention()


def _proj_attention(hn, w, g, dil, seq):
    t = hn.shape[0]
    l_sub = seq // dil
    tr = min(ATTN_ROWS, l_sub)
    n_tiles = t // tr
    pair = HEADS_PER_STEP * HEAD_DIM
    n_pairs = N_HEADS // HEADS_PER_STEP
    bias = jnp.asarray(_attn_bias_table(g, dil))

    def tile_done(i, hp):
        return (jnp.maximum(i - 1, 0), 0)

    def w_spec(part):
        first = (part * N_GROUPS + g) * n_pairs
        return pl.BlockSpec((D_MODEL, pair), lambda i, hp: (0, first + hp))

    kern = functools.partial(_proj_attn_kernel, tiles_per_seq=l_sub // tr)
    return pl.pallas_call(
        kern,
        grid=(n_tiles + 1, n_pairs),
        in_specs=[pl.BlockSpec((tr, D_MODEL), lambda i, hp: (jnp.minimum(i, n_tiles - 1), 0)),
                  w_spec(0), w_spec(1), w_spec(2),
                  pl.BlockSpec((2, HEADS_PER_STEP, BLOCK, 2 * BLOCK),
                               lambda i, hp: (0, hp, 0, 0))],
        out_specs=[pl.BlockSpec((tr, pair),
                                lambda i, hp: (jnp.maximum(i - 1, 0), jnp.where(i == 0, 0, hp))),
                   pl.BlockSpec((tr, HEAD_DIM), tile_done),
                   pl.BlockSpec((tr, HEAD_DIM), tile_done)],
        out_shape=[jax.ShapeDtypeStruct((t, ATTN_WIDTH), BF16),
                   jax.ShapeDtypeStruct((t, HEAD_DIM), F32),
                   jax.ShapeDtypeStruct((t, HEAD_DIM), F32)],
        scratch_shapes=[pltpu.VMEM((n_pairs, tr, 3 * pair), BF16),
                        pltpu.VMEM((n_pairs, BLOCK, 2 * pair), BF16)],
        compiler_params=_params("arbitrary", "arbitrary"),
        name=f"proj_attn{g}",
    )(hn, w, w, w, bias)


def _finish(y_ref, x_ref, wout_ref, gpost_ref, gnext_ref, perm_refs, xo_ref, hn_refs):
    out = jnp.dot(y_ref[...], wout_ref[...], preferred_element_type=F32)
    xn = x_ref[...] + _rms(out, gpost_ref[...])
    xo_ref[...] = xn
    if hn_refs:
        _emit_hn(_rms(xn, gnext_ref[...]), perm_refs, hn_refs)


def _attn_out_kernel(o0_ref, o1_ref, o2_ref, m0_ref, m1_ref, m2_ref, l0_ref, l1_ref, l2_ref,
                     hn_ref, x_ref, wgate_ref, wout_ref, gpost_ref, gnext_ref, *refs):
    perm_refs, xo_ref, hn_refs, y_ref = (), refs[0], refs[1:-1], refs[-1]

    @pl.when(pl.program_id(0) == 0)
    def _():
        y_ref[...] = jnp.zeros_like(y_ref)

    _finish(y_ref, x_ref, wout_ref, gpost_ref, gnext_ref, perm_refs, xo_ref, hn_refs)
    ms = [m0_ref[...], m1_ref[...], m2_ref[...]]
    m = jnp.maximum(jnp.maximum(ms[0], ms[1]), ms[2])
    es = [jnp.exp2(t - m) for t in ms]
    den = es[0] * l0_ref[...] + es[1] * l1_ref[...] + es[2] * l2_ref[...]
    ws = [e / den for e in es]
    hn = hn_ref[...]
    pair = MERGE_HEADS * HEAD_DIM
    for hp in range(N_HEADS // MERGE_HEADS):
        ps = slice(hp * pair, (hp + 1) * pair)
        gate = jnp.dot(hn, wgate_ref[:, ps], preferred_element_type=F32)
        os_ = [o_ref[:, ps].astype(F32) for o_ref in (o0_ref, o1_ref, o2_ref)]
        for hh in range(MERGE_HEADS):
            h = hp * MERGE_HEADS + hh
            sub = slice(hh * HEAD_DIM, (hh + 1) * HEAD_DIM)
            acc = ws[0][:, h:h + 1] * os_[0][:, sub]
            for g in range(1, N_GROUPS):
                acc = acc + ws[g][:, h:h + 1] * os_[g][:, sub]
            y_ref[:, h * HEAD_DIM:(h + 1) * HEAD_DIM] = (
                acc * _silu(gate[:, sub])).astype(y_ref.dtype)


def _attn_epilogue(os_, stats, hn, x2, wgate, wout, gpost, gnext, tiles, emit):
    tm = tiles.tm
    assert emit == 1
    perms, perm_specs = [], []
    hn_specs, hn_shapes = tiles.hn_out(emit, 1)
    return pl.pallas_call(
        _attn_out_kernel,
        grid=(tiles.n + 1,),
        in_specs=[tiles.tok(ATTN_WIDTH, 0)] * N_GROUPS
                 + [tiles.tok(HEAD_DIM, 0)] * (2 * N_GROUPS)
                 + [tiles.tok(D_MODEL, 0), tiles.tok(D_MODEL, 1),
                    _const_spec((D_MODEL, ATTN_WIDTH)),
                    _const_spec((ATTN_WIDTH, D_MODEL)), _const_spec((1, D_MODEL)),
                    _const_spec((1, D_MODEL))] + perm_specs,
        out_specs=[tiles.tok(D_MODEL, 1)] + hn_specs,
        out_shape=[jax.ShapeDtypeStruct(x2.shape, F32)] + hn_shapes,
        scratch_shapes=[pltpu.VMEM((tm, ATTN_WIDTH), BF16)],
        compiler_params=_params("arbitrary"),
        name="attn_epilogue",
    )(*os_, *stats, hn, x2, wgate, wout, gpost.reshape(1, -1), gnext.reshape(1, -1), *perms)


def _conv_mix_kernel(hn_ref, wh_ref, wb_ref, wc_ref, wg_ref, cw_ref, y_ref, halo_ref, *,
                     tiles_per_seq):
    i, j = pl.program_id(0), pl.program_id(1)
    tr = hn_ref.shape[0]

    @pl.when((i == 0) & (j == 0))
    def _():
        halo_ref[...] = jnp.zeros_like(halo_ref)

    hn = hn_ref[...]
    first = (i % tiles_per_seq) == 0
    for c0 in range(0, y_ref.shape[1], MXU_COLS):
        cs = slice(c0, c0 + MXU_COLS)
        u = _project(hn, wc_ref, cols=cs) * _project(hn, wh_ref, cols=cs)
        ue = jnp.concatenate([jnp.where(first, 0.0, halo_ref[j, :, cs]), u], axis=0)
        um1 = ue[HALO_ROWS - 1:HALO_ROWS - 1 + tr]
        um2 = ue[HALO_ROWS - 2:HALO_ROWS - 2 + tr]
        halo_ref[j, :, cs] = u[tr - HALO_ROWS:tr]
        cw = cw_ref[:, cs]
        conv = cw[0:1] * um2 + cw[1:2] * um1 + cw[2:3] * u
        y = _project(hn, wb_ref, cols=cs) * conv * _silu(_project(hn, wg_ref, cols=cs))
        y_ref[:, cs] = y.astype(y_ref.dtype)


def _conv_mix(hn, w, conv_w, seq):
    t = hn.shape[0]
    tr = min(PROJ_ROWS, seq)
    n_chunks = D_MODEL // CONV_COLS

    def w_spec(part):
        return pl.BlockSpec((D_MODEL, CONV_COLS), lambda i, j: (0, part * n_chunks + j))

    kern = functools.partial(_conv_mix_kernel, tiles_per_seq=seq // tr)
    return pl.pallas_call(
        kern,
        grid=(t // tr, n_chunks),
        in_specs=[pl.BlockSpec((tr, D_MODEL), lambda i, j: (i, 0)),
                  w_spec(0), w_spec(1), w_spec(2), w_spec(3),
                  pl.BlockSpec((CONV_K, CONV_COLS), lambda i, j: (0, j))],
        out_specs=pl.BlockSpec((tr, CONV_COLS), lambda i, j: (i, j)),
        out_shape=jax.ShapeDtypeStruct((t, D_MODEL), BF16),
        scratch_shapes=[pltpu.VMEM((n_chunks, HALO_ROWS, CONV_COLS), F32)],
        compiler_params=_params("arbitrary", "arbitrary"),
        name="conv_mix",
    )(hn, w, w, w, w, conv_w)


def _out_kernel(y_ref, x_ref, wout_ref, gpost_ref, gnext_ref, *refs, n_perm):
    perm_refs, xo_ref, hn_refs = refs[:n_perm], refs[n_perm], refs[n_perm + 1:]
    _finish(y_ref, x_ref, wout_ref, gpost_ref, gnext_ref, perm_refs, xo_ref, hn_refs)


def _out_epilogue(y, x2, wout, gpost, gnext, tiles, emit):
    perms, perm_specs = _perm_inputs(emit >= 2)
    hn_specs, hn_shapes = tiles.hn_out(emit, 0)
    kern = functools.partial(_out_kernel, n_perm=len(perms))
    return pl.pallas_call(
        kern,
        grid=(tiles.n,),
        in_specs=[tiles.tok(D_MODEL, 0), tiles.tok(D_MODEL, 0),
                  _const_spec((D_MODEL, D_MODEL)), _const_spec((1, D_MODEL)),
                  _const_spec((1, D_MODEL))] + perm_specs,
        out_specs=[tiles.tok(D_MODEL, 0)] + hn_specs,
        out_shape=[jax.ShapeDtypeStruct(x2.shape, F32)] + hn_shapes,
        compiler_params=_params("parallel"),
        name="out_epilogue",
    )(y, x2, wout, gpost.reshape(1, -1), gnext.reshape(1, -1), *perms)


def _attention_layer(x2, hns, w_in, w_out, gpost, gnext, tiles, emit):
    t = x2.shape[0]
    batch, seq = tiles.batch, tiles.seq
    os_, ms, ls = [], [], []
    for g, dil in enumerate(DILATIONS):
        o, m, l = _proj_attention(hns[g].reshape(t, D_MODEL), w_in, g, dil, seq)
        os_.append(o if dil == 1 else jnp.swapaxes(
            o.reshape(batch, dil, seq // dil, ATTN_WIDTH), 1, 2).reshape(t, ATTN_WIDTH))

        def to_token_order(a, dil=dil):
            a = a.reshape(batch, dil, seq // dil, HEAD_DIM)
            return jnp.swapaxes(a, 1, 2).reshape(t, HEAD_DIM)

        ms.append(to_token_order(m))
        ls.append(to_token_order(l))
    w_gate = w_in[:, 3 * N_GROUPS * ATTN_WIDTH:].astype(BF16)
    return _attn_epilogue(os_, ms + ls, hns[0], x2, w_gate, w_out.astype(BF16), gpost, gnext,
                          tiles, emit)


def _conv_layer(x2, hn, w_in, conv_w, w_out, gpost, gnext, tiles, emit):
    y = _conv_mix(hn, w_in, conv_w, tiles.seq)
    out_tiles = _Tiles(tiles.batch, tiles.seq, OUT_TOKENS)
    return _out_epilogue(y, x2, w_out.astype(BF16), gpost, gnext, out_tiles, emit)


def kernel(x, l0_norm_pre, l0_w_in, l0_w_out, l0_norm_post, l1_norm_pre, l1_w_in, l1_conv_w, l1_w_out, l1_norm_post, l2_norm_pre, l2_w_in, l2_w_out, l2_norm_post, l3_norm_pre, l3_w_in, l3_conv_w, l3_w_out, l3_norm_post):
    batch, seq, d = x.shape
    assert d == D_MODEL and seq % (DILATIONS[-1] * BLOCK) == 0
    tiles = _Tiles(batch, seq, EPILOGUE_TOKENS)
    x2 = x.reshape(batch * seq, D_MODEL)
    hns = _prenorm(x2, l0_norm_pre, _Tiles(batch, seq, OUT_TOKENS))
    x2, hn = _attention_layer(x2, hns, l0_w_in, l0_w_out, l0_norm_post, l1_norm_pre, tiles, 1)
    x2, *hns = _conv_layer(x2, hn, l1_w_in, l1_conv_w, l1_w_out, l1_norm_post, l2_norm_pre, tiles, 2)
    x2, hn = _attention_layer(x2, hns, l2_w_in, l2_w_out, l2_norm_post, l3_norm_pre, tiles, 1)
    (x2,) = _conv_layer(x2, hn, l3_w_in, l3_conv_w, l3_w_out, l3_norm_post, l3_norm_pre, tiles, 0)
    return x2.reshape(batch, seq, D_MODEL)
```

```python
import functools

import jax
import jax.numpy as jnp
import numpy as np
from jax import lax
from jax.experimental import pallas as pl
from jax.experimental.pallas import tpu as pltpu

D_MODEL = 2048
HEAD_DIM = 128
N_HEADS = 16
ATTN_WIDTH = N_HEADS * HEAD_DIM
DILATIONS = (1, 4, 16)
N_GROUPS = len(DILATIONS)
BLOCK = 128
CONV_K = 3
NORM_EPS = 1e-6
HALO_ROWS = 8
V7X_VMEM_LIMIT_BYTES = 56 * 1024 * 1024
MXU_COLS = 256
PERM_TOKENS = 256
EPILOGUE_TOKENS = PERM_TOKENS
OUT_TOKENS = 2 * PERM_TOKENS
PROJ_ROWS = 1024
CONV_COLS = 512
ATTN_ROWS = 1024
HEADS_PER_STEP = 4
MERGE_HEADS = 2
LOG2E = 1.4426950408889634
Q_SCALE = HEAD_DIM ** -0.5 * LOG2E

BF16 = jnp.bfloat16
F32 = jnp.float32


def _alibi_slopes():
    n = N_GROUPS * N_HEADS
    s = 2.0 ** (-8.0 * np.arange(1, n + 1) / n)
    return s.reshape(N_GROUPS, N_HEADS).astype(np.float32)


def _to_token_order(tm, dil):
    p = np.zeros((tm, tm), np.float32)
    u, r = np.meshgrid(np.arange(tm // dil), np.arange(dil), indexing="ij")
    p[(u * dil + r).ravel(), (r * (tm // dil) + u).ravel()] = 1.0
    return p


def _params(*semantics):
    return pltpu.CompilerParams(dimension_semantics=semantics,
                                vmem_limit_bytes=V7X_VMEM_LIMIT_BYTES)


def _rms(v, g):
    return v * lax.rsqrt(jnp.mean(v * v, axis=-1, keepdims=True) + NORM_EPS) * g


def _silu(g):
    half = 0.5 * g
    return half + half * jnp.tanh(half)


class _Tiles:
    def __init__(self, batch, seq, tm):
        self.batch, self.seq, self.tm = batch, seq, tm
        self.per_seq = seq // tm
        self.n = batch * self.per_seq

    def tile(self, s, lag):
        return jnp.clip(s - lag, 0, self.n - 1)

    def tok(self, width, lag, col=0):
        return pl.BlockSpec((self.tm, width), lambda s: (self.tile(s, lag), col))

    def res(self, width, dil, lag):
        def index(s):
            j = self.tile(s, lag)
            return (j // self.per_seq, 0, j % self.per_seq, 0)
        return pl.BlockSpec((None, dil, self.tm // dil, width), index)

    def hn_out(self, emit, lag):
        specs, shapes = [], []
        if emit >= 1:
            specs.append(self.tok(D_MODEL, lag))
            shapes.append(jax.ShapeDtypeStruct((self.batch * self.seq, D_MODEL), BF16))
        if emit >= 2:
            for dil in DILATIONS[1:]:
                specs.append(self.res(D_MODEL, dil, lag))
                shapes.append(jax.ShapeDtypeStruct(
                    (self.batch, dil, self.seq // dil, D_MODEL), BF16))
        return specs, shapes


def _const_spec(shape):
    return pl.BlockSpec(shape, lambda s: (0,) * len(shape), pipeline_mode=pl.Buffered(1))


def _perm_inputs(wanted):
    if not wanted:
        return [], []
    mats = []
    for dil in DILATIONS[1:]:
        p = _to_token_order(PERM_TOKENS, dil)
        mats += [p, p.T]
    specs = [_const_spec((PERM_TOKENS, PERM_TOKENS))] * len(mats)
    return [jnp.asarray(m, BF16) for m in mats], specs


def _emit_hn(hn, perm_refs, out_refs):
    if not out_refs:
        return
    hb = hn.astype(BF16)
    out_refs[0][...] = hb
    for k, o_ref in enumerate(out_refs[1:]):
        to_res = perm_refs[2 * k + 1][...]
        dil = o_ref.shape[0]
        rows = PERM_TOKENS // dil
        for t0 in range(0, hb.shape[0], PERM_TOKENS):
            res = jnp.dot(to_res, hb[t0:t0 + PERM_TOKENS],
                          preferred_element_type=F32).astype(BF16)
            for r in range(dil):
                o_ref[r, t0 // dil:t0 // dil + rows] = res[r * rows:(r + 1) * rows]


def _norm_kernel(x_ref, g_ref, o_ref):
    _emit_hn(_rms(x_ref[...], g_ref[...]), (), (o_ref,))


def _prenorm(x2, g, tiles):
    out_specs, out_shapes = tiles.hn_out(1, 0)
    return pl.pallas_call(
        _norm_kernel,
        grid=(tiles.n,),
        in_specs=[tiles.tok(D_MODEL, 0), _const_spec((1, D_MODEL))],
        out_specs=out_specs,
        out_shape=out_shapes,
        compiler_params=_params("parallel"),
        name="prenorm",
    )(x2, g.reshape(1, D_MODEL))


def _residue_rows(batch, seq, dil):
    idx = np.arange(batch * seq, dtype=np.int32).reshape(batch, seq // dil, dil)
    return np.swapaxes(idx, 1, 2).reshape(-1)


def _to_residue_order(a, batch, seq, dil):
    return jnp.take(a, jnp.asarray(_residue_rows(batch, seq, dil)), axis=0)


def _from_residue_order(a, batch, seq, dil):
    inverse = np.argsort(_residue_rows(batch, seq, dil)).astype(np.int32)
    return jnp.take(a, jnp.asarray(inverse), axis=0)


def _project(hn, w_ref, scale=None, cols=slice(None)):
    w = w_ref[:, cols] if scale is None else w_ref[:, cols] * scale
    return jnp.dot(hn, w.astype(BF16), preferred_element_type=F32)


def _attn_bias_table(g, dil):
    qi = np.arange(BLOCK)[:, None]
    kj = np.arange(2 * BLOCK)[None, :]
    dist = qi + BLOCK - kj
    in_window = (dist >= 0) & (dist <= BLOCK)
    slopes = _alibi_slopes()[g].astype(np.float64) * LOG2E
    bias = -(slopes[:, None, None] * (dil * dist)[None].astype(np.float64))
    general = np.where(in_window[None], bias, -np.inf)
    first = np.where((kj >= BLOCK)[None], general, -np.inf)
    return np.stack([first, general]).astype(np.float32)


def _attend(q, kp, kc, vp, vc, bias):
    k = jnp.concatenate([kp, kc], axis=0)
    v = jnp.concatenate([vp, vc], axis=0)
    s = lax.dot_general(q, k, (((1,), (1,)), ((), ())), preferred_element_type=F32) + bias
    m = jnp.max(s, axis=-1, keepdims=True)
    p = jnp.exp2(s - m)
    l = jnp.sum(p, axis=-1, keepdims=True)
    return jnp.dot(p.astype(BF16), v, preferred_element_type=F32), m, l


def _proj_attn_kernel(hn_ref, wq_ref, wk_ref, wv_ref, bias_ref, o_ref, m_ref, l_ref, slab_ref,
                      halo_ref, *, tiles_per_seq):
    i, hp = pl.program_id(0), pl.program_id(1)
    last = pl.num_programs(0) - 1
    tr = hn_ref.shape[0]
    pair = HEADS_PER_STEP * HEAD_DIM

    @pl.when((i == 0) & (hp == 0))
    def _():
        halo_ref[...] = jnp.zeros_like(halo_ref)

    @pl.when(hp == 0)
    def _():
        m_ref[...] = jnp.zeros_like(m_ref)
        l_ref[...] = jnp.ones_like(l_ref)

    def cols(part, hh):
        return slice(part * pair + hh * HEAD_DIM, part * pair + (hh + 1) * HEAD_DIM)

    def attention():
        first = ((i - 1) % tiles_per_seq) == 0
        lane = lax.broadcasted_iota(jnp.int32, (BLOCK, HEAD_DIM), 1)
        for qb in range(tr // BLOCK):
            rows = slice(qb * BLOCK, (qb + 1) * BLOCK)
            prows = slice((qb - 1) * BLOCK, qb * BLOCK)
            m_t, l_t = m_ref[rows, :], l_ref[rows, :]
            for hh in range(HEADS_PER_STEP):
                if qb == 0:
                    kp, vp = halo_ref[hp, :, cols(0, hh)], halo_ref[hp, :, cols(1, hh)]
                    bias = jnp.where(first, bias_ref[0, hh], bias_ref[1, hh])
                else:
                    kp, vp = slab_ref[hp, prows, cols(1, hh)], slab_ref[hp, prows, cols(2, hh)]
                    bias = bias_ref[1, hh]
                o, m, l = _attend(slab_ref[hp, rows, cols(0, hh)], kp,
                                  slab_ref[hp, rows, cols(1, hh)], vp,
                                  slab_ref[hp, rows, cols(2, hh)], bias)
                o_ref[rows, hh * HEAD_DIM:(hh + 1) * HEAD_DIM] = o.astype(o_ref.dtype)
                head = hp * HEADS_PER_STEP + hh
                m_t = jnp.where(lane == head, m, m_t)
                l_t = jnp.where(lane == head, l, l_t)
            m_ref[rows, :] = m_t
            l_ref[rows, :] = l_t
        halo_ref[hp] = slab_ref[hp, tr - BLOCK:tr, pair:3 * pair]

    def projection():
        hn = hn_ref[...]
        for part, (w_ref, scale) in enumerate(((wq_ref, Q_SCALE), (wk_ref, None), (wv_ref, None))):
            slab_ref[hp, :, part * pair:(part + 1) * pair] = _project(
                hn, w_ref, scale).astype(slab_ref.dtype)

    @pl.when(i == 0)
    def _():
        projection()

    @pl.when((i > 0) & (i < last))
    def _():
        attention()
        projection()

    @pl.when(i == last)
    def _():
        attention()


def _proj_attention(hn, w, g, dil, seq):
    t = hn.shape[0]
    l_sub = seq // dil
    tr = min(ATTN_ROWS, l_sub)
    n_tiles = t // tr
    pair = HEADS_PER_STEP * HEAD_DIM
    n_pairs = N_HEADS // HEADS_PER_STEP
    bias = jnp.asarray(_attn_bias_table(g, dil))

    def tile_done(i, hp):
        return (jnp.maximum(i - 1, 0), 0)

    def w_spec(part):
        first = (part * N_GROUPS + g) * n_pairs
        return pl.BlockSpec((D_MODEL, pair), lambda i, hp: (0, first + hp))

    kern = functools.partial(_proj_attn_kernel, tiles_per_seq=l_sub // tr)
    return pl.pallas_call(
        kern,
        grid=(n_tiles + 1, n_pairs),
        in_specs=[pl.BlockSpec((tr, D_MODEL), lambda i, hp: (jnp.minimum(i, n_tiles - 1), 0)),
                  w_spec(0), w_spec(1), w_spec(2),
                  pl.BlockSpec((2, HEADS_PER_STEP, BLOCK, 2 * BLOCK),
                               lambda i, hp: (0, hp, 0, 0))],
        out_specs=[pl.BlockSpec((tr, pair),
                                lambda i, hp: (jnp.maximum(i - 1, 0), jnp.where(i == 0, 0, hp))),
                   pl.BlockSpec((tr, HEAD_DIM), tile_done),
                   pl.BlockSpec((tr, HEAD_DIM), tile_done)],
        out_shape=[jax.ShapeDtypeStruct((t, ATTN_WIDTH), BF16),
                   jax.ShapeDtypeStruct((t, HEAD_DIM), F32),
                   jax.ShapeDtypeStruct((t, HEAD_DIM), F32)],
        scratch_shapes=[pltpu.VMEM((n_pairs, tr, 3 * pair), BF16),
                        pltpu.VMEM((n_pairs, BLOCK, 2 * pair), BF16)],
        compiler_params=_params("arbitrary", "arbitrary"),
        name=f"proj_attn{g}",
    )(hn, w, w, w, bias)


def _finish(y_ref, x_ref, wout_ref, gpost_ref, gnext_ref, perm_refs, xo_ref, hn_refs):
    out = jnp.dot(y_ref[...], wout_ref[...], preferred_element_type=F32)
    xn = x_ref[...] + _rms(out, gpost_ref[...])
    xo_ref[...] = xn
    if hn_refs:
        _emit_hn(_rms(xn, gnext_ref[...]), perm_refs, hn_refs)


def _attn_out_kernel(o0_ref, o1_ref, o2_ref, m0_ref, m1_ref, m2_ref, l0_ref, l1_ref, l2_ref,
                     hn_ref, x_ref, wgate_ref, wout_ref, gpost_ref, gnext_ref, *refs):
    perm_refs, xo_ref, hn_refs, y_ref = refs[:4], refs[4], refs[5:-1], refs[-1]
    tm = x_ref.shape[0]

    @pl.when(pl.program_id(0) == 0)
    def _():
        y_ref[...] = jnp.zeros_like(y_ref)

    _finish(y_ref, x_ref, wout_ref, gpost_ref, gnext_ref, perm_refs, xo_ref, hn_refs)
    ms = [m0_ref[...], m1_ref[...], m2_ref[...]]
    m = jnp.maximum(jnp.maximum(ms[0], ms[1]), ms[2])
    es = [jnp.exp2(t - m) for t in ms]
    den = es[0] * l0_ref[...] + es[1] * l1_ref[...] + es[2] * l2_ref[...]
    ws = [e / den for e in es]
    hn = hn_ref[...]
    pair = MERGE_HEADS * HEAD_DIM
    for hp in range(N_HEADS // MERGE_HEADS):
        ps = slice(hp * pair, (hp + 1) * pair)
        gate = jnp.dot(hn, wgate_ref[:, ps], preferred_element_type=F32)
        os_ = [o0_ref[:, ps].astype(F32), o1_ref[:, ps].astype(F32),
               jnp.dot(perm_refs[2][...], o2_ref[:, :, ps].reshape(tm, pair),
                       preferred_element_type=F32)]
        for hh in range(MERGE_HEADS):
            h = hp * MERGE_HEADS + hh
            sub = slice(hh * HEAD_DIM, (hh + 1) * HEAD_DIM)
            acc = ws[0][:, h:h + 1] * os_[0][:, sub]
            for g in range(1, N_GROUPS):
                acc = acc + ws[g][:, h:h + 1] * os_[g][:, sub]
            y_ref[:, h * HEAD_DIM:(h + 1) * HEAD_DIM] = (
                acc * _silu(gate[:, sub])).astype(y_ref.dtype)


def _attn_epilogue(os_, stats, hn, x2, wgate, wout, gpost, gnext, tiles, emit):
    tm = tiles.tm
    assert tm == PERM_TOKENS
    perms, perm_specs = _perm_inputs(True)
    hn_specs, hn_shapes = tiles.hn_out(emit, 1)
    return pl.pallas_call(
        _attn_out_kernel,
        grid=(tiles.n + 1,),
        in_specs=[tiles.tok(ATTN_WIDTH, 0), tiles.tok(ATTN_WIDTH, 0),
                  tiles.res(ATTN_WIDTH, DILATIONS[2], 0)]
                 + [tiles.tok(HEAD_DIM, 0)] * (2 * N_GROUPS)
                 + [tiles.tok(D_MODEL, 0), tiles.tok(D_MODEL, 1),
                    _const_spec((D_MODEL, ATTN_WIDTH)),
                    _const_spec((ATTN_WIDTH, D_MODEL)), _const_spec((1, D_MODEL)),
                    _const_spec((1, D_MODEL))] + perm_specs,
        out_specs=[tiles.tok(D_MODEL, 1)] + hn_specs,
        out_shape=[jax.ShapeDtypeStruct(x2.shape, F32)] + hn_shapes,
        scratch_shapes=[pltpu.VMEM((tm, ATTN_WIDTH), BF16)],
        compiler_params=_params("arbitrary"),
        name="attn_epilogue",
    )(*os_, *stats, hn, x2, wgate, wout, gpost.reshape(1, -1), gnext.reshape(1, -1), *perms)


def _conv_mix_kernel(hn_ref, wh_ref, wb_ref, wc_ref, wg_ref, cw_ref, y_ref, halo_ref, *,
                     tiles_per_seq):
    i, j = pl.program_id(0), pl.program_id(1)
    tr = hn_ref.shape[0]

    @pl.when((i == 0) & (j == 0))
    def _():
        halo_ref[...] = jnp.zeros_like(halo_ref)

    hn = hn_ref[...]
    first = (i % tiles_per_seq) == 0
    for c0 in range(0, y_ref.shape[1], MXU_COLS):
        cs = slice(c0, c0 + MXU_COLS)
        u = _project(hn, wc_ref, cols=cs) * _project(hn, wh_ref, cols=cs)
        ue = jnp.concatenate([jnp.where(first, 0.0, halo_ref[j, :, cs]), u], axis=0)
        um1 = ue[HALO_ROWS - 1:HALO_ROWS - 1 + tr]
        um2 = ue[HALO_ROWS - 2:HALO_ROWS - 2 + tr]
        halo_ref[j, :, cs] = u[tr - HALO_ROWS:tr]
        cw = cw_ref[:, cs]
        conv = cw[0:1] * um2 + cw[1:2] * um1 + cw[2:3] * u
        y = _project(hn, wb_ref, cols=cs) * conv * _silu(_project(hn, wg_ref, cols=cs))
        y_ref[:, cs] = y.astype(y_ref.dtype)


def _conv_mix(hn, w, conv_w, seq):
    t = hn.shape[0]
    tr = min(PROJ_ROWS, seq)
    n_chunks = D_MODEL // CONV_COLS

    def w_spec(part):
        return pl.BlockSpec((D_MODEL, CONV_COLS), lambda i, j: (0, part * n_chunks + j))

    kern = functools.partial(_conv_mix_kernel, tiles_per_seq=seq // tr)
    return pl.pallas_call(
        kern,
        grid=(t // tr, n_chunks),
        in_specs=[pl.BlockSpec((tr, D_MODEL), lambda i, j: (i, 0)),
                  w_spec(0), w_spec(1), w_spec(2), w_spec(3),
                  pl.BlockSpec((CONV_K, CONV_COLS), lambda i, j: (0, j))],
        out_specs=pl.BlockSpec((tr, CONV_COLS), lambda i, j: (i, j)),
        out_shape=jax.ShapeDtypeStruct((t, D_MODEL), BF16),
        scratch_shapes=[pltpu.VMEM((n_chunks, HALO_ROWS, CONV_COLS), F32)],
        compiler_params=_params("arbitrary", "arbitrary"),
        name="conv_mix",
    )(hn, w, w, w, w, conv_w)


def _out_kernel(y_ref, x_ref, wout_ref, gpost_ref, gnext_ref, *refs, n_perm):
    perm_refs, xo_ref, hn_refs = refs[:n_perm], refs[n_perm], refs[n_perm + 1:]
    _finish(y_ref, x_ref, wout_ref, gpost_ref, gnext_ref, perm_refs, xo_ref, hn_refs)


def _out_epilogue(y, x2, wout, gpost, gnext, tiles, emit):
    perms, perm_specs = _perm_inputs(emit >= 2)
    hn_specs, hn_shapes = tiles.hn_out(emit, 0)
    kern = functools.partial(_out_kernel, n_perm=len(perms))
    return pl.pallas_call(
        kern,
        grid=(tiles.n,),
        in_specs=[tiles.tok(D_MODEL, 0), tiles.tok(D_MODEL, 0),
                  _const_spec((D_MODEL, D_MODEL)), _const_spec((1, D_MODEL)),
                  _const_spec((1, D_MODEL))] + perm_specs,
        out_specs=[tiles.tok(D_MODEL, 0)] + hn_specs,
        out_shape=[jax.ShapeDtypeStruct(x2.shape, F32)] + hn_shapes,
        compiler_params=_params("parallel"),
        name="out_epilogue",
    )(y, x2, wout, gpost.reshape(1, -1), gnext.reshape(1, -1), *perms)


def _attention_layer(x2, hn, w_in, w_out, gpost, gnext, tiles, emit):
    t = x2.shape[0]
    batch, seq = tiles.batch, tiles.seq
    hns = [hn] + [_to_residue_order(hn, batch, seq, dil) for dil in DILATIONS[1:]]
    os_, ms, ls = [], [], []
    for g, dil in enumerate(DILATIONS):
        o, m, l = _proj_attention(hns[g], w_in, g, dil, seq)
        if g == 1:
            o = _from_residue_order(o, batch, seq, dil)
        os_.append(o if g < N_GROUPS - 1 else o.reshape(batch, dil, seq // dil, ATTN_WIDTH))

        def to_token_order(a, dil=dil):
            a = a.reshape(batch, dil, seq // dil, HEAD_DIM)
            return jnp.swapaxes(a, 1, 2).reshape(t, HEAD_DIM)

        ms.append(to_token_order(m))
        ls.append(to_token_order(l))
    w_gate = w_in[:, 3 * N_GROUPS * ATTN_WIDTH:].astype(BF16)
    return _attn_epilogue(os_, ms + ls, hns[0], x2, w_gate, w_out.astype(BF16), gpost, gnext,
                          tiles, emit)


def _conv_layer(x2, hn, w_in, conv_w, w_out, gpost, gnext, tiles, emit):
    y = _conv_mix(hn, w_in, conv_w, tiles.seq)
    out_tiles = _Tiles(tiles.batch, tiles.seq, OUT_TOKENS)
    return _out_epilogue(y, x2, w_out.astype(BF16), gpost, gnext, out_tiles, emit)


def kernel(x, l0_norm_pre, l0_w_in, l0_w_out, l0_norm_post, l1_norm_pre, l1_w_in, l1_conv_w, l1_w_out, l1_norm_post, l2_norm_pre, l2_w_in, l2_w_out, l2_norm_post, l3_norm_pre, l3_w_in, l3_conv_w, l3_w_out, l3_norm_post):
    batch, seq, d = x.shape
    assert d == D_MODEL and seq % (DILATIONS[-1] * BLOCK) == 0
    tiles = _Tiles(batch, seq, EPILOGUE_TOKENS)
    x2 = x.reshape(batch * seq, D_MODEL)
    (hn,) = _prenorm(x2, l0_norm_pre, _Tiles(batch, seq, OUT_TOKENS))
    x2, hn = _attention_layer(x2, hn, l0_w_in, l0_w_out, l0_norm_post, l1_norm_pre, tiles, 1)
    x2, hn = _conv_layer(x2, hn, l1_w_in, l1_conv_w, l1_w_out, l1_norm_post, l2_norm_pre, tiles, 1)
    x2, hn = _attention_layer(x2, hn, l2_w_in, l2_w_out, l2_norm_post, l3_norm_pre, tiles, 1)
    (x2,) = _conv_layer(x2, hn, l3_w_in, l3_conv_w, l3_w_out, l3_norm_post, l3_norm_pre, tiles, 0)
    return x2.reshape(batch, seq, D_MODEL)
```

```python
import functools

import jax
import jax.numpy as jnp
import numpy as np
from jax import lax
from jax.experimental import pallas as pl
from jax.experimental.pallas import tpu as pltpu

D_MODEL = 2048
HEAD_DIM = 128
N_HEADS = 16
ATTN_WIDTH = N_HEADS * HEAD_DIM
DILATIONS = (1, 4, 16)
N_GROUPS = len(DILATIONS)
BLOCK = 128
CONV_K = 3
NORM_EPS = 1e-6
HALO_ROWS = 8
V7X_VMEM_LIMIT_BYTES = 56 * 1024 * 1024
MXU_COLS = 256
PERM_TOKENS = 256
EPILOGUE_TOKENS = PERM_TOKENS
OUT_TOKENS = 2 * PERM_TOKENS
PROJ_ROWS = 1024
CONV_COLS = 512
ATTN_ROWS = 1024
HEADS_PER_STEP = 4
MERGE_HEADS = 2
LOG2E = 1.4426950408889634
Q_SCALE = HEAD_DIM ** -0.5 * LOG2E

BF16 = jnp.bfloat16
F32 = jnp.float32


def _alibi_slopes():
    n = N_GROUPS * N_HEADS
    s = 2.0 ** (-8.0 * np.arange(1, n + 1) / n)
    return s.reshape(N_GROUPS, N_HEADS).astype(np.float32)


def _to_token_order(tm, dil):
    p = np.zeros((tm, tm), np.float32)
    u, r = np.meshgrid(np.arange(tm // dil), np.arange(dil), indexing="ij")
    p[(u * dil + r).ravel(), (r * (tm // dil) + u).ravel()] = 1.0
    return p


def _params(*semantics):
    return pltpu.CompilerParams(dimension_semantics=semantics,
                                vmem_limit_bytes=V7X_VMEM_LIMIT_BYTES)


def _rms(v, g):
    return v * lax.rsqrt(jnp.mean(v * v, axis=-1, keepdims=True) + NORM_EPS) * g


def _silu(g):
    half = 0.5 * g
    return half + half * jnp.tanh(half)


class _Tiles:
    def __init__(self, batch, seq, tm):
        self.batch, self.seq, self.tm = batch, seq, tm
        self.per_seq = seq // tm
        self.n = batch * self.per_seq

    def tile(self, s, lag):
        return jnp.clip(s - lag, 0, self.n - 1)

    def tok(self, width, lag, col=0):
        return pl.BlockSpec((self.tm, width), lambda s: (self.tile(s, lag), col))

    def res(self, width, dil, lag):
        def index(s):
            j = self.tile(s, lag)
            return (j // self.per_seq, 0, j % self.per_seq, 0)
        return pl.BlockSpec((None, dil, self.tm // dil, width), index)

    def hn_out(self, emit, lag):
        specs, shapes = [], []
        if emit >= 1:
            specs.append(self.tok(D_MODEL, lag))
            shapes.append(jax.ShapeDtypeStruct((self.batch * self.seq, D_MODEL), BF16))
        if emit >= 2:
            for dil in DILATIONS[1:]:
                specs.append(self.res(D_MODEL, dil, lag))
                shapes.append(jax.ShapeDtypeStruct(
                    (self.batch, dil, self.seq // dil, D_MODEL), BF16))
        return specs, shapes


def _const_spec(shape):
    return pl.BlockSpec(shape, lambda s: (0,) * len(shape), pipeline_mode=pl.Buffered(1))


def _perm_inputs(wanted):
    if not wanted:
        return [], []
    mats = []
    for dil in DILATIONS[1:]:
        p = _to_token_order(PERM_TOKENS, dil)
        mats += [p, p.T]
    specs = [_const_spec((PERM_TOKENS, PERM_TOKENS))] * len(mats)
    return [jnp.asarray(m, BF16) for m in mats], specs


def _emit_hn(hn, perm_refs, out_refs):
    if not out_refs:
        return
    hb = hn.astype(BF16)
    out_refs[0][...] = hb
    for k, o_ref in enumerate(out_refs[1:]):
        to_res = perm_refs[2 * k + 1][...]
        dil = o_ref.shape[0]
        rows = PERM_TOKENS // dil
        for t0 in range(0, hb.shape[0], PERM_TOKENS):
            res = jnp.dot(to_res, hb[t0:t0 + PERM_TOKENS],
                          preferred_element_type=F32).astype(BF16)
            for r in range(dil):
                o_ref[r, t0 // dil:t0 // dil + rows] = res[r * rows:(r + 1) * rows]


def _norm_kernel(x_ref, g_ref, *refs):
    perm_refs, out_refs = refs[:4], refs[4:]
    _emit_hn(_rms(x_ref[...], g_ref[...]), perm_refs, out_refs)


def _prenorm(x2, g, tiles):
    perms, perm_specs = _perm_inputs(True)
    out_specs, out_shapes = tiles.hn_out(2, 0)
    return pl.pallas_call(
        _norm_kernel,
        grid=(tiles.n,),
        in_specs=[tiles.tok(D_MODEL, 0), _const_spec((1, D_MODEL))] + perm_specs,
        out_specs=out_specs,
        out_shape=out_shapes,
        compiler_params=_params("parallel"),
        name="prenorm",
    )(x2, g.reshape(1, D_MODEL), *perms)


def _project(hn, w_ref, scale=None, cols=slice(None)):
    w = w_ref[:, cols] if scale is None else w_ref[:, cols] * scale
    return jnp.dot(hn, w.astype(BF16), preferred_element_type=F32)


def _attn_bias_table(g, dil):
    qi = np.arange(BLOCK)[:, None]
    kj = np.arange(2 * BLOCK)[None, :]
    dist = qi + BLOCK - kj
    in_window = (dist >= 0) & (dist <= BLOCK)
    slopes = _alibi_slopes()[g].astype(np.float64) * LOG2E
    bias = -(slopes[:, None, None] * (dil * dist)[None].astype(np.float64))
    general = np.where(in_window[None], bias, -np.inf)
    first = np.where((kj >= BLOCK)[None], general, -np.inf)
    return np.stack([first, general]).astype(np.float32)


def _attend(q, kp, kc, vp, vc, bias):
    k = jnp.concatenate([kp, kc], axis=0)
    v = jnp.concatenate([vp, vc], axis=0)
    kt = jnp.transpose(k.astype(F32)).astype(BF16)
    s = jnp.dot(q, kt, preferred_element_type=F32) + bias
    m = jnp.max(s, axis=-1, keepdims=True)
    p = jnp.exp2(s - m)
    l = jnp.sum(p, axis=-1, keepdims=True)
    return jnp.dot(p.astype(BF16), v, preferred_element_type=F32), m, l


def _proj_attn_kernel(hn_ref, wq_ref, wk_ref, wv_ref, bias_ref, o_ref, m_ref, l_ref, slab_ref,
                      halo_ref, *, tiles_per_seq):
    i, hp = pl.program_id(0), pl.program_id(1)
    last = pl.num_programs(0) - 1
    tr = hn_ref.shape[0]
    pair = HEADS_PER_STEP * HEAD_DIM

    @pl.when((i == 0) & (hp == 0))
    def _():
        halo_ref[...] = jnp.zeros_like(halo_ref)

    @pl.when(hp == 0)
    def _():
        m_ref[...] = jnp.zeros_like(m_ref)
        l_ref[...] = jnp.ones_like(l_ref)

    def cols(part, hh):
        return slice(part * pair + hh * HEAD_DIM, part * pair + (hh + 1) * HEAD_DIM)

    def attention():
        first = ((i - 1) % tiles_per_seq) == 0
        lane = lax.broadcasted_iota(jnp.int32, (BLOCK, HEAD_DIM), 1)
        for qb in range(tr // BLOCK):
            rows = slice(qb * BLOCK, (qb + 1) * BLOCK)
            prows = slice((qb - 1) * BLOCK, qb * BLOCK)
            m_t, l_t = m_ref[rows, :], l_ref[rows, :]
            for hh in range(HEADS_PER_STEP):
                if qb == 0:
                    kp, vp = halo_ref[hp, :, cols(0, hh)], halo_ref[hp, :, cols(1, hh)]
                    bias = jnp.where(first, bias_ref[0, hh], bias_ref[1, hh])
                else:
                    kp, vp = slab_ref[hp, prows, cols(1, hh)], slab_ref[hp, prows, cols(2, hh)]
                    bias = bias_ref[1, hh]
                o, m, l = _attend(slab_ref[hp, rows, cols(0, hh)], kp,
                                  slab_ref[hp, rows, cols(1, hh)], vp,
                                  slab_ref[hp, rows, cols(2, hh)], bias)
                o_ref[rows, hh * HEAD_DIM:(hh + 1) * HEAD_DIM] = o.astype(o_ref.dtype)
                head = hp * HEADS_PER_STEP + hh
                m_t = jnp.where(lane == head, m, m_t)
                l_t = jnp.where(lane == head, l, l_t)
            m_ref[rows, :] = m_t
            l_ref[rows, :] = l_t
        halo_ref[hp] = slab_ref[hp, tr - BLOCK:tr, pair:3 * pair]

    def projection():
        hn = hn_ref[...]
        for part, (w_ref, scale) in enumerate(((wq_ref, Q_SCALE), (wk_ref, None), (wv_ref, None))):
            slab_ref[hp, :, part * pair:(part + 1) * pair] = _project(
                hn, w_ref, scale).astype(slab_ref.dtype)

    @pl.when(i == 0)
    def _():
        projection()

    @pl.when((i > 0) & (i < last))
    def _():
        attention()
        projection()

    @pl.when(i == last)
    def _():
        attention()


def _proj_attention(hn, w, g, dil, seq):
    t = hn.shape[0]
    l_sub = seq // dil
    tr = min(ATTN_ROWS, l_sub)
    n_tiles = t // tr
    pair = HEADS_PER_STEP * HEAD_DIM
    n_pairs = N_HEADS // HEADS_PER_STEP
    bias = jnp.asarray(_attn_bias_table(g, dil))

    def tile_done(i, hp):
        return (jnp.maximum(i - 1, 0), 0)

    def w_spec(part):
        first = (part * N_GROUPS + g) * n_pairs
        return pl.BlockSpec((D_MODEL, pair), lambda i, hp: (0, first + hp))

    kern = functools.partial(_proj_attn_kernel, tiles_per_seq=l_sub // tr)
    return pl.pallas_call(
        kern,
        grid=(n_tiles + 1, n_pairs),
        in_specs=[pl.BlockSpec((tr, D_MODEL), lambda i, hp: (jnp.minimum(i, n_tiles - 1), 0)),
                  w_spec(0), w_spec(1), w_spec(2),
                  pl.BlockSpec((2, HEADS_PER_STEP, BLOCK, 2 * BLOCK),
                               lambda i, hp: (0, hp, 0, 0))],
        out_specs=[pl.BlockSpec((tr, pair),
                                lambda i, hp: (jnp.maximum(i - 1, 0), jnp.where(i == 0, 0, hp))),
                   pl.BlockSpec((tr, HEAD_DIM), tile_done),
                   pl.BlockSpec((tr, HEAD_DIM), tile_done)],
        out_shape=[jax.ShapeDtypeStruct((t, ATTN_WIDTH), BF16),
                   jax.ShapeDtypeStruct((t, HEAD_DIM), F32),
                   jax.ShapeDtypeStruct((t, HEAD_DIM), F32)],
        scratch_shapes=[pltpu.VMEM((n_pairs, tr, 3 * pair), BF16),
                        pltpu.VMEM((n_pairs, BLOCK, 2 * pair), BF16)],
        compiler_params=_params("arbitrary", "arbitrary"),
        name=f"proj_attn{g}",
    )(hn, w, w, w, bias)


def _finish(y_ref, x_ref, wout_ref, gpost_ref, gnext_ref, perm_refs, xo_ref, hn_refs):
    out = jnp.dot(y_ref[...], wout_ref[...], preferred_element_type=F32)
    xn = x_ref[...] + _rms(out, gpost_ref[...])
    xo_ref[...] = xn
    if hn_refs:
        _emit_hn(_rms(xn, gnext_ref[...]), perm_refs, hn_refs)


def _attn_out_kernel(o0_ref, o1_ref, o2_ref, m0_ref, m1_ref, m2_ref, l0_ref, l1_ref, l2_ref,
                     hn_ref, x_ref, wgate_ref, wout_ref, gpost_ref, gnext_ref, *refs):
    perm_refs, xo_ref, hn_refs, y_ref = refs[:4], refs[4], refs[5:-1], refs[-1]
    tm = x_ref.shape[0]

    @pl.when(pl.program_id(0) == 0)
    def _():
        y_ref[...] = jnp.zeros_like(y_ref)

    _finish(y_ref, x_ref, wout_ref, gpost_ref, gnext_ref, perm_refs, xo_ref, hn_refs)
    ms = [m0_ref[...], m1_ref[...], m2_ref[...]]
    m = jnp.maximum(jnp.maximum(ms[0], ms[1]), ms[2])
    es = [jnp.exp2(t - m) for t in ms]
    den = es[0] * l0_ref[...] + es[1] * l1_ref[...] + es[2] * l2_ref[...]
    ws = [e / den for e in es]
    hn = hn_ref[...]
    pair = MERGE_HEADS * HEAD_DIM
    for hp in range(N_HEADS // MERGE_HEADS):
        ps = slice(hp * pair, (hp + 1) * pair)
        gate = jnp.dot(hn, wgate_ref[:, ps], preferred_element_type=F32)
        os_ = [o0_ref[:, ps].astype(F32)]
        for k, o_ref in enumerate((o1_ref, o2_ref)):
            res = o_ref[:, :, ps].reshape(tm, pair)
            os_.append(jnp.dot(perm_refs[2 * k][...], res, preferred_element_type=F32))
        for hh in range(MERGE_HEADS):
            h = hp * MERGE_HEADS + hh
            sub = slice(hh * HEAD_DIM, (hh + 1) * HEAD_DIM)
            acc = ws[0][:, h:h + 1] * os_[0][:, sub]
            for g in range(1, N_GROUPS):
                acc = acc + ws[g][:, h:h + 1] * os_[g][:, sub]
            y_ref[:, h * HEAD_DIM:(h + 1) * HEAD_DIM] = (
                acc * _silu(gate[:, sub])).astype(y_ref.dtype)


def _attn_epilogue(os_, stats, hn, x2, wgate, wout, gpost, gnext, tiles, emit):
    tm = tiles.tm
    assert tm == PERM_TOKENS
    perms, perm_specs = _perm_inputs(True)
    hn_specs, hn_shapes = tiles.hn_out(emit, 1)
    return pl.pallas_call(
        _attn_out_kernel,
        grid=(tiles.n + 1,),
        in_specs=[tiles.tok(ATTN_WIDTH, 0), tiles.res(ATTN_WIDTH, DILATIONS[1], 0),
                  tiles.res(ATTN_WIDTH, DILATIONS[2], 0)]
                 + [tiles.tok(HEAD_DIM, 0)] * (2 * N_GROUPS)
                 + [tiles.tok(D_MODEL, 0), tiles.tok(D_MODEL, 1),
                    _const_spec((D_MODEL, ATTN_WIDTH)),
                    _const_spec((ATTN_WIDTH, D_MODEL)), _const_spec((1, D_MODEL)),
                    _const_spec((1, D_MODEL))] + perm_specs,
        out_specs=[tiles.tok(D_MODEL, 1)] + hn_specs,
        out_shape=[jax.ShapeDtypeStruct(x2.shape, F32)] + hn_shapes,
        scratch_shapes=[pltpu.VMEM((tm, ATTN_WIDTH), BF16)],
        compiler_params=_params("arbitrary"),
        name="attn_epilogue",
    )(*os_, *stats, hn, x2, wgate, wout, gpost.reshape(1, -1), gnext.reshape(1, -1), *perms)


def _conv_mix_kernel(hn_ref, wh_ref, wb_ref, wc_ref, wg_ref, cw_ref, y_ref, halo_ref, *,
                     tiles_per_seq):
    i, j = pl.program_id(0), pl.program_id(1)
    tr = hn_ref.shape[0]

    @pl.when((i == 0) & (j == 0))
    def _():
        halo_ref[...] = jnp.zeros_like(halo_ref)

    hn = hn_ref[...]
    first = (i % tiles_per_seq) == 0
    for c0 in range(0, y_ref.shape[1], MXU_COLS):
        cs = slice(c0, c0 + MXU_COLS)
        u = _project(hn, wc_ref, cols=cs) * _project(hn, wh_ref, cols=cs)
        ue = jnp.concatenate([jnp.where(first, 0.0, halo_ref[j, :, cs]), u], axis=0)
        um1 = ue[HALO_ROWS - 1:HALO_ROWS - 1 + tr]
        um2 = ue[HALO_ROWS - 2:HALO_ROWS - 2 + tr]
        halo_ref[j, :, cs] = u[tr - HALO_ROWS:tr]
        cw = cw_ref[:, cs]
        conv = cw[0:1] * um2 + cw[1:2] * um1 + cw[2:3] * u
        y = _project(hn, wb_ref, cols=cs) * conv * _silu(_project(hn, wg_ref, cols=cs))
        y_ref[:, cs] = y.astype(y_ref.dtype)


def _conv_mix(hn, w, conv_w, seq):
    t = hn.shape[0]
    tr = min(PROJ_ROWS, seq)
    n_chunks = D_MODEL // CONV_COLS

    def w_spec(part):
        return pl.BlockSpec((D_MODEL, CONV_COLS), lambda i, j: (0, part * n_chunks + j))

    kern = functools.partial(_conv_mix_kernel, tiles_per_seq=seq // tr)
    return pl.pallas_call(
        kern,
        grid=(t // tr, n_chunks),
        in_specs=[pl.BlockSpec((tr, D_MODEL), lambda i, j: (i, 0)),
                  w_spec(0), w_spec(1), w_spec(2), w_spec(3),
                  pl.BlockSpec((CONV_K, CONV_COLS), lambda i, j: (0, j))],
        out_specs=pl.BlockSpec((tr, CONV_COLS), lambda i, j: (i, j)),
        out_shape=jax.ShapeDtypeStruct((t, D_MODEL), BF16),
        scratch_shapes=[pltpu.VMEM((n_chunks, HALO_ROWS, CONV_COLS), F32)],
        compiler_params=_params("arbitrary", "arbitrary"),
        name="conv_mix",
    )(hn, w, w, w, w, conv_w)


def _out_kernel(y_ref, x_ref, wout_ref, gpost_ref, gnext_ref, *refs, n_perm):
    perm_refs, xo_ref, hn_refs = refs[:n_perm], refs[n_perm], refs[n_perm + 1:]
    _finish(y_ref, x_ref, wout_ref, gpost_ref, gnext_ref, perm_refs, xo_ref, hn_refs)


def _out_epilogue(y, x2, wout, gpost, gnext, tiles, emit):
    perms, perm_specs = _perm_inputs(emit >= 2)
    hn_specs, hn_shapes = tiles.hn_out(emit, 0)
    kern = functools.partial(_out_kernel, n_perm=len(perms))
    return pl.pallas_call(
        kern,
        grid=(tiles.n,),
        in_specs=[tiles.tok(D_MODEL, 0), tiles.tok(D_MODEL, 0),
                  _const_spec((D_MODEL, D_MODEL)), _const_spec((1, D_MODEL)),
                  _const_spec((1, D_MODEL))] + perm_specs,
        out_specs=[tiles.tok(D_MODEL, 0)] + hn_specs,
        out_shape=[jax.ShapeDtypeStruct(x2.shape, F32)] + hn_shapes,
        compiler_params=_params("parallel"),
        name="out_epilogue",
    )(y, x2, wout, gpost.reshape(1, -1), gnext.reshape(1, -1), *perms)


def _attention_layer(x2, hns, w_in, w_out, gpost, gnext, tiles, emit):
    t = x2.shape[0]
    batch, seq = tiles.batch, tiles.seq
    os_, ms, ls = [], [], []
    for g, dil in enumerate(DILATIONS):
        o, m, l = _proj_attention(hns[g].reshape(t, D_MODEL), w_in, g, dil, seq)
        os_.append(o if dil == 1 else o.reshape(batch, dil, seq // dil, ATTN_WIDTH))

        def to_token_order(a, dil=dil):
            a = a.reshape(batch, dil, seq // dil, HEAD_DIM)
            return jnp.swapaxes(a, 1, 2).reshape(t, HEAD_DIM)

        ms.append(to_token_order(m))
        ls.append(to_token_order(l))
    w_gate = w_in[:, 3 * N_GROUPS * ATTN_WIDTH:].astype(BF16)
    return _attn_epilogue(os_, ms + ls, hns[0], x2, w_gate, w_out.astype(BF16), gpost, gnext,
                          tiles, emit)


def _conv_layer(x2, hn, w_in, conv_w, w_out, gpost, gnext, tiles, emit):
    y = _conv_mix(hn, w_in, conv_w, tiles.seq)
    out_tiles = _Tiles(tiles.batch, tiles.seq, OUT_TOKENS)
    return _out_epilogue(y, x2, w_out.astype(BF16), gpost, gnext, out_tiles, emit)


def kernel(x, l0_norm_pre, l0_w_in, l0_w_out, l0_norm_post, l1_norm_pre, l1_w_in, l1_conv_w, l1_w_out, l1_norm_post, l2_norm_pre, l2_w_in, l2_w_out, l2_norm_post, l3_norm_pre, l3_w_in, l3_conv_w, l3_w_out, l3_norm_post):
    batch, seq, d = x.shape
    assert d == D_MODEL and seq % (DILATIONS[-1] * BLOCK) == 0
    tiles = _Tiles(batch, seq, EPILOGUE_TOKENS)
    x2 = x.reshape(batch * seq, D_MODEL)
    hns = _prenorm(x2, l0_norm_pre, _Tiles(batch, seq, OUT_TOKENS))
    x2, hn = _attention_layer(x2, hns, l0_w_in, l0_w_out, l0_norm_post, l1_norm_pre, tiles, 1)
    x2, *hns = _conv_layer(x2, hn, l1_w_in, l1_conv_w, l1_w_out, l1_norm_post, l2_norm_pre, tiles, 2)
    x2, hn = _attention_layer(x2, hns, l2_w_in, l2_w_out, l2_norm_post, l3_norm_pre, tiles, 1)
    (x2,) = _conv_layer(x2, hn, l3_w_in, l3_conv_w, l3_w_out, l3_norm_post, l3_norm_pre, tiles, 0)
    return x2.reshape(batch, seq, D_MODEL)
```
